```python
import jax, jax.numpy as jnp
from jax import lax
import numpy as np

D_MODEL = 1024
BATCH = 8
SEQ = 2048
DEPTH = 4

N_MIX = 2
N_RWKV = (DEPTH + 1) // 2
N_ATTN = DEPTH // 2
N_VRES = max(N_RWKV - 1, 0)
RWKV_HEAD = 64
RWKV_HEADS = D_MODEL // RWKV_HEAD
DECAY_LORA = 64
ICLR_LORA = 64
VRES_LORA = 32
GATE_LORA = 128
GN_EPS = 64e-5
HEAD_DIM = 64
N_Q_HEADS = D_MODEL // HEAD_DIM
N_KV_HEADS = 4
GROUP = N_Q_HEADS // N_KV_HEADS
QKV_DIM = (N_Q_HEADS + 2 * N_KV_HEADS) * HEAD_DIM
WINDOW = 128
BLOCK = 128
ROPE_THETA = 500000.0
ROPE_DIM = HEAD_DIM // 4
D_FF = ((8 * D_MODEL // 3 + 255) // 256) * 256
RMS_EPS = 1e-6

kernel_name = "hybrid_rwkv7_swa_sink_swiglu_sandwich"


def rmsnorm(x, g):
    xf = x.astype(jnp.float32)
    y = xf * lax.rsqrt(jnp.mean(xf * xf, axis=-1, keepdims=True) + RMS_EPS)
    return (y * g.astype(jnp.float32)).astype(x.dtype)


def wkv7_scan(r, decay, k, v, a, b):
    Bsz, _, H, N = r.shape
    xs = tuple(jnp.moveaxis(t, 1, 0) for t in (r, decay, k, v, a, b))

    def step(S, inp):
        r_t, w_t, k_t, v_t, a_t, b_t = inp
        sa = jnp.einsum('bhvk,bhk->bhv', S, a_t)
        S = S * w_t[:, :, None, :] + sa[..., None] * b_t[:, :, None, :] + v_t[..., None] * k_t[:, :, None, :]
        y = jnp.einsum('bhvk,bhk->bhv', S, r_t)
        return S, y

    S0 = jnp.zeros((Bsz, H, N, N), jnp.float32)
    _, ys = lax.scan(step, S0, xs)
    return jnp.moveaxis(ys, 0, 1)


def rwkv7_time_mix(x, v_first, x_mix, w_rkv, w0, w1, w2, a0, a1, a2, g1, g2,
                   k_k, k_a, r_k, lnx_w, lnx_b, w_o, vres):
    Bsz, T, C = x.shape
    H, N = RWKV_HEADS, RWKV_HEAD
    f32 = jnp.float32
    x_prev = jnp.pad(x, ((0, 0), (1, 0), (0, 0)))[:, :-1]
    xx = x_prev - x
    xm = x[:, :, None, :] + xx[:, :, None, :] * x_mix
    rkv = jnp.einsum('btjc,jcd->btjd', xm[:, :, :3], w_rkv)
    r, k, v = rkv[:, :, 0], rkv[:, :, 1], rkv[:, :, 2]
    xv, xw, xa, xg = xm[:, :, 2], xm[:, :, 3], xm[:, :, 4], xm[:, :, 5]

    w_log = -jax.nn.softplus(-(w0 + jnp.tanh(xw @ w1) @ w2).astype(f32)) - 0.5
    if vres is None:
        v_first = v
    else:
        v0, v1, v2 = vres
        v = v + (v_first - v) * jax.nn.sigmoid(v0 + (xv @ v1) @ v2)
    a = jax.nn.sigmoid(a0 + (xa @ a1) @ a2)
    g = jax.nn.sigmoid(xg @ g1) @ g2

    kk = (k * k_k).reshape(Bsz, T, H, N).astype(f32)
    kk = kk / jnp.maximum(jnp.sqrt(jnp.sum(kk * kk, -1, keepdims=True)), 1e-12)
    k = k * (1 + (a - 1) * k_a)

    rh = r.reshape(Bsz, T, H, N).astype(f32)
    kh = k.reshape(Bsz, T, H, N).astype(f32)
    vh = v.reshape(Bsz, T, H, N).astype(f32)
    ah = a.reshape(Bsz, T, H, N).astype(f32)
    decay = jnp.exp(-jnp.exp(w_log)).reshape(Bsz, T, H, N)
    y = wkv7_scan(rh, decay, kh, vh, -kk, kk * ah)

    mu = jnp.mean(y, -1, keepdims=True)
    var = jnp.mean(jnp.square(y - mu), -1, keepdims=True)
    y = ((y - mu) * lax.rsqrt(var + GN_EPS)).reshape(Bsz, T, C)
    y = y * lnx_w.astype(f32) + lnx_b.astype(f32)
    bonus = jnp.sum(rh * kh * r_k.astype(f32), -1, keepdims=True) * vh
    y = (y + bonus.reshape(Bsz, T, C)).astype(x.dtype)
    return (y * g) @ w_o, v_first


def rope_tables(T):
    pos = jnp.arange(T, dtype=jnp.float32)
    inv_freq = ROPE_THETA ** (-jnp.arange(0, ROPE_DIM, 2, dtype=jnp.float32) / ROPE_DIM)
    ang = pos[:, None] * inv_freq[None, :]
    return jnp.cos(ang), jnp.sin(ang)


def apply_partial_rope(t, cos, sin):
    tr = t[..., :ROPE_DIM].astype(jnp.float32)
    t1, t2 = tr[..., :ROPE_DIM // 2], tr[..., ROPE_DIM // 2:]
    c, s = cos[None, :, None, :], sin[None, :, None, :]
    rot = jnp.concatenate([t1 * c - t2 * s, t2 * c + t1 * s], axis=-1)
    return jnp.concatenate([rot.astype(t.dtype), t[..., ROPE_DIM:]], axis=-1)


def swa_sink_attention(x, w_qkv, b_qkv, sinks, w_o, b_o, cos, sin):
    Bsz, T, C = x.shape
    nb = T // BLOCK
    qkv = x @ w_qkv + b_qkv
    q = qkv[..., :N_Q_HEADS * HEAD_DIM].reshape(Bsz, T, N_Q_HEADS, HEAD_DIM)
    k = qkv[..., N_Q_HEADS * HEAD_DIM:(N_Q_HEADS + N_KV_HEADS) * HEAD_DIM].reshape(Bsz, T, N_KV_HEADS, HEAD_DIM)
    v = qkv[..., (N_Q_HEADS + N_KV_HEADS) * HEAD_DIM:].reshape(Bsz, T, N_KV_HEADS, HEAD_DIM)
    q = apply_partial_rope(q, cos, sin)
    k = apply_partial_rope(k, cos, sin)

    q = q.reshape(Bsz, nb, BLOCK, N_KV_HEADS, GROUP, HEAD_DIM)
    k = k.reshape(Bsz, nb, BLOCK, N_KV_HEADS, HEAD_DIM)
    v = v.reshape(Bsz, nb, BLOCK, N_KV_HEADS, HEAD_DIM)

    def with_prev(t):
        prev = jnp.pad(t, ((0, 0), (1, 0), (0, 0), (0, 0), (0, 0)))[:, :-1]
        return jnp.concatenate([prev, t], axis=2)

    kb, vb = with_prev(k), with_prev(v)
    s = jnp.einsum('bnqhgd,bnkhd->bnhgqk', q, kb).astype(jnp.float32) * (HEAD_DIM ** -0.5)

    qi = jnp.arange(BLOCK)[:, None]
    kj = jnp.arange(2 * BLOCK)[None, :]
    rel = qi + BLOCK - kj
    band = (rel >= 0) & (rel < WINDOW)
    valid = band[None] & ((jnp.arange(nb)[:, None, None] > 0) | (kj[None] >= BLOCK))
    s = jnp.where(valid[None, :, None, None], s, -jnp.inf)

    sink = sinks.astype(jnp.float32).reshape(N_KV_HEADS, GROUP)[None, None, :, :, None, None]
    m = jnp.maximum(jnp.max(s, -1, keepdims=True), sink)
    p = jnp.exp(s - m)
    p = p / (jnp.sum(p, -1, keepdims=True) + jnp.exp(sink - m))
    o = jnp.einsum('bnhgqk,bnkhd->bnqhgd', p.astype(x.dtype), vb).reshape(Bsz, T, N_Q_HEADS * HEAD_DIM)
    return o @ w_o + b_o


def swiglu_ffn(x, w_gate_up, w_down):
    h = x @ w_gate_up
    gate, up = h[..., :D_FF], h[..., D_FF:]
    return (jax.nn.silu(gate) * up) @ w_down


def setup_inputs(seed: int = 0) -> dict:
    key = jax.random.key(seed)
    ks = iter(jax.random.split(key, 40))
    D, H, N = D_MODEL, RWKV_HEADS, RWKV_HEAD
    nrm = lambda shape, scale: scale * jax.random.normal(next(ks), shape, jnp.float32)
    return {
        "x": nrm((BATCH, SEQ, D), 1.0),
        "norm_g": 1.0 + nrm((DEPTH, 4, D), 0.05),
        "rwkv_x_mix": jax.random.uniform(next(ks), (N_RWKV, 6, D), jnp.float32),
        "rwkv_w_rkv": nrm((N_RWKV, 3, D, D), D ** -0.5),
        "rwkv_w0": jax.random.uniform(next(ks), (N_RWKV, D), jnp.float32, -6.5, -1.5),
        "rwkv_w1": nrm((N_RWKV, D, DECAY_LORA), D ** -0.5),
        "rwkv_w2": nrm((N_RWKV, DECAY_LORA, D), 0.1 * DECAY_LORA ** -0.5),
        "rwkv_a0": nrm((N_RWKV, D), 0.1),
        "rwkv_a1": nrm((N_RWKV, D, ICLR_LORA), D ** -0.5),
        "rwkv_a2": nrm((N_RWKV, ICLR_LORA, D), 0.5 * ICLR_LORA ** -0.5),
        "rwkv_v0": 1.0 + nrm((N_VRES, D), 0.1),
        "rwkv_v1": nrm((N_VRES, D, VRES_LORA), D ** -0.5),
        "rwkv_v2": nrm((N_VRES, VRES_LORA, D), 0.5 * VRES_LORA ** -0.5),
        "rwkv_g1": nrm((N_RWKV, D, GATE_LORA), D ** -0.5),
        "rwkv_g2": nrm((N_RWKV, GATE_LORA, D), GATE_LORA ** -0.5),
        "rwkv_k_k": 0.85 + nrm((N_RWKV, D), 0.05),
        "rwkv_k_a": 1.0 + nrm((N_RWKV, D), 0.05),
        "rwkv_r_k": -0.04 + nrm((N_RWKV, H, N), 0.05),
        "rwkv_lnx_w": 1.0 + nrm((N_RWKV, D), 0.05),
        "rwkv_lnx_b": nrm((N_RWKV, D), 0.02),
        "rwkv_w_o": nrm((N_RWKV, D, D), D ** -0.5),
        "attn_w_qkv": nrm((N_ATTN, D, QKV_DIM), D ** -0.5),
        "attn_b_qkv": nrm((N_ATTN, QKV_DIM), 0.02),
        "attn_sinks": nrm((N_ATTN, N_Q_HEADS), 1.0),
        "attn_w_o": nrm((N_ATTN, N_Q_HEADS * HEAD_DIM, D), (N_Q_HEADS * HEAD_DIM) ** -0.5),
        "attn_b_o": nrm((N_ATTN, D), 0.02),
        "ffn_w_gate_up": nrm((DEPTH, D, 2 * D_FF), D ** -0.5),
        "ffn_w_down": nrm((DEPTH, D_FF, D), D_FF ** -0.5),
    }


def reference(x, norm_g, rwkv_x_mix, rwkv_w_rkv, rwkv_w0, rwkv_w1, rwkv_w2, rwkv_a0, rwkv_a1,
              rwkv_a2, rwkv_v0, rwkv_v1, rwkv_v2, rwkv_g1, rwkv_g2, rwkv_k_k, rwkv_k_a, rwkv_r_k,
              rwkv_lnx_w, rwkv_lnx_b, rwkv_w_o, attn_w_qkv, attn_b_qkv, attn_sinks, attn_w_o,
              attn_b_o, ffn_w_gate_up, ffn_w_down):
    cos, sin = rope_tables(x.shape[1])
    v_first = None
    for i in range(DEPTH):
        j = i // N_MIX
        h = rmsnorm(x, norm_g[i, 0])
        if i % N_MIX == 0:
            vres = None if j == 0 else (rwkv_v0[j - 1], rwkv_v1[j - 1], rwkv_v2[j - 1])
            h, v_first = rwkv7_time_mix(
                h, v_first, rwkv_x_mix[j], rwkv_w_rkv[j], rwkv_w0[j], rwkv_w1[j], rwkv_w2[j],
                rwkv_a0[j], rwkv_a1[j], rwkv_a2[j], rwkv_g1[j], rwkv_g2[j], rwkv_k_k[j],
                rwkv_k_a[j], rwkv_r_k[j], rwkv_lnx_w[j], rwkv_lnx_b[j], rwkv_w_o[j], vres)
        else:
            h = swa_sink_attention(h, attn_w_qkv[j], attn_b_qkv[j], attn_sinks[j],
                                   attn_w_o[j], attn_b_o[j], cos, sin)
        x = x + rmsnorm(h, norm_g[i, 1])
        h = swiglu_ffn(rmsnorm(x, norm_g[i, 2]), ffn_w_gate_up[i], ffn_w_down[i])
        x = x + rmsnorm(h, norm_g[i, 3])
    return x
```

```python
import functools
import math

import jax
import jax.numpy as jnp
from jax import lax
from jax.experimental import pallas as pl
from jax.experimental.pallas import tpu as pltpu

D_MODEL = 1024
HEAD = 64
N_HEADS = D_MODEL // HEAD
N_KV_HEADS = 4
GROUP = N_HEADS // N_KV_HEADS
KV_DIM = N_KV_HEADS * HEAD
QKV_DIM = D_MODEL + 2 * KV_DIM
WINDOW = 128
BLOCK = 128
ROPE_THETA = 500000.0
ROPE_DIM = HEAD // 4
D_FF = 2816
RMS_EPS = 1e-6
GN_EPS = 64e-5

LANES = 128
PAIR = LANES // HEAD
WKV_CHUNK = 64
WKV_PAIRS = 4
TOKEN_TILE = 256
FFN_TILE = 512
FFN_CHUNK = 1408
VMEM_LIMIT = 56 * 1024 * 1024

BF16 = jnp.bfloat16
F32 = jnp.float32


def _dot(a, b):
    return jnp.dot(a.astype(BF16), b.astype(BF16), preferred_element_type=F32)


def _dot_nt(a, b):
    return lax.dot_general(a.astype(BF16), b.astype(BF16), (((1,), (1,)), ((), ())),
                           preferred_element_type=F32)


def _dot_tn(a, b):
    return lax.dot_general(a.astype(BF16), b.astype(BF16), (((0,), (0,)), ((), ())),
                           preferred_element_type=F32)


def _dot_split(a, b01):
    hi = a.astype(BF16)
    lo = (a - hi.astype(F32)).astype(BF16)
    return (jnp.dot(hi, b01, preferred_element_type=F32)
            + jnp.dot(lo, b01, preferred_element_type=F32))


def _cumsum_rows(x, tri01):
    hi = x.astype(BF16)
    rest = x - hi.astype(F32)
    mid = rest.astype(BF16)
    lo = (rest - mid.astype(F32)).astype(BF16)
    return (jnp.dot(tri01, hi, preferred_element_type=F32)
            + jnp.dot(tri01, mid, preferred_element_type=F32)
            + jnp.dot(tri01, lo, preferred_element_type=F32))


def _rms(x, g):
    return x * lax.rsqrt(jnp.mean(x * x, axis=-1, keepdims=True) + RMS_EPS) * g


def _sigmoid(x):
    return 1.0 / (1.0 + jnp.exp(-x))


def _const_spec(shape):
    zeros = (0,) * len(shape)
    return pl.BlockSpec(shape, lambda *_: zeros, pipeline_mode=pl.Buffered(1))


def _params(*sem):
    return pltpu.CompilerParams(dimension_semantics=sem, vmem_limit_bytes=VMEM_LIMIT)


def _rwkv_proj_kernel(has_vres, *refs):
    if has_vres:
        (x_ref, halo_ref, g_ref, mix_ref, wrkv_ref, w0_ref, w1_ref, w2_ref, a0_ref, a1_ref,
         a2_ref, g1_ref, g2_ref, vf_ref, v0_ref, v1_ref, v2_ref,
         r_ref, lw_ref, k_ref, v_ref, as_ref, gate_ref) = refs
    else:
        (x_ref, halo_ref, g_ref, mix_ref, wrkv_ref, w0_ref, w1_ref, w2_ref, a0_ref, a1_ref,
         a2_ref, g1_ref, g2_ref,
         r_ref, lw_ref, k_ref, v_ref, as_ref, gate_ref) = refs
    i = pl.program_id(1)
    g = g_ref[...]
    h = _rms(x_ref[0], g)
    prev_last = _rms(halo_ref[0][7:8, :], g)
    prev_last = jnp.where(i == 0, 0.0, prev_last)
    row = lax.broadcasted_iota(jnp.int32, h.shape, 0)
    h_prev = jnp.where(row == 0, prev_last, pltpu.roll(h, 1, 0))
    xx = h_prev - h
    mix = mix_ref[...]
    xr, xk, xv, xw, xa, xg = (h + xx * mix[j:j + 1, :] for j in range(6))

    r_ref[0] = _dot(xr, wrkv_ref[0])
    k_ref[0] = _dot(xk, wrkv_ref[1])
    v = _dot(xv, wrkv_ref[2])
    if has_vres:
        mixv = _sigmoid(v0_ref[...] + _dot(_dot(xv, v1_ref[...]), v2_ref[...]))
        v = v + (vf_ref[0] - v) * mixv
    v_ref[0] = v
    z = w0_ref[...] + _dot(jnp.tanh(_dot(xw, w1_ref[...])), w2_ref[...])
    lw_ref[0] = (-math.exp(-0.5)) * _sigmoid(z)
    as_ref[0] = _sigmoid(a0_ref[...] + _dot(_dot(xa, a1_ref[...]), a2_ref[...]))
    gate_ref[0] = _dot(_sigmoid(_dot(xg, g1_ref[...])), g2_ref[...])


def _rwkv_proj(x, g, mix, wrkv, w0, w1, w2, a0, a1, a2, g1, g2, vres):
    B, T, D = x.shape
    tm = TOKEN_TILE
    tile = pl.BlockSpec((1, tm, D), lambda b, i: (b, i, 0))
    halo = pl.BlockSpec((1, 8, D), lambda b, i: (b, jnp.maximum(i * (tm // 8) - 1, 0), 0))
    row = lambda a: a.reshape(1, -1)
    args = [x, x, row(g), mix, wrkv.astype(BF16), row(w0), w1.astype(BF16), w2.astype(BF16),
            row(a0), a1.astype(BF16), a2.astype(BF16), g1.astype(BF16), g2.astype(BF16)]
    specs = [tile, halo] + [_const_spec(a.shape) for a in args[2:]]
    if vres is not None:
        v_first, v0, v1, v2 = vres
        extra = [row(v0), v1.astype(BF16), v2.astype(BF16)]
        args += [v_first] + extra
        specs += [tile] + [_const_spec(a.shape) for a in extra]
    out = jax.ShapeDtypeStruct((B, T, D), F32)
    return pl.pallas_call(
        functools.partial(_rwkv_proj_kernel, vres is not None),
        grid=(B, T // tm),
        in_specs=specs,
        out_specs=[tile] * 6,
        out_shape=[out] * 6,
        compiler_params=_params("parallel", "arbitrary"),
        name="rwkv_proj",
    )(*args)


def _pair_rows(x, lane_lo):
    return jnp.concatenate([jnp.where(lane_lo, x, 0.0), jnp.where(lane_lo, 0.0, x)], axis=0)


def _wkv_kernel(r_ref, lw_ref, k_ref, v_ref, as_ref, kk_ref, ka_ref, rk_ref, lnw_ref, lnb_ref,
                y_ref, state_ref):
    L = WKV_CHUNK
    L2 = 2 * L

    @pl.when(pl.program_id(2) == 0)
    def _():
        state_ref[...] = jnp.zeros_like(state_ref)

    ti = lax.broadcasted_iota(jnp.int32, (L, L), 0)
    tj = lax.broadcasted_iota(jnp.int32, (L, L), 1)
    tri_incl = (ti >= tj).astype(BF16)
    pi = lax.broadcasted_iota(jnp.int32, (L2, L2), 0)
    pj = lax.broadcasted_iota(jnp.int32, (L2, L2), 1)
    same_head = (pi // L) == (pj // L)
    strict_lower = same_head & (pi > pj)
    lower = same_head & (pi >= pj)
    lane_lo = lax.broadcasted_iota(jnp.int32, (L, LANES), 1) < HEAD
    si = lax.broadcasted_iota(jnp.int32, (LANES, LANES), 0)
    sj = lax.broadcasted_iota(jnp.int32, (LANES, LANES), 1)
    head_ones = ((si // HEAD) == (sj // HEAD)).astype(BF16)

    for p in range(WKV_PAIRS):
        sl = slice(p * LANES, (p + 1) * LANES)
        r = r_ref[0, :, sl]
        lw = lw_ref[0, :, sl]
        k = k_ref[0, :, sl]
        v = v_ref[0, :, sl]
        asig = as_ref[0, :, sl]

        kk = k * kk_ref[:, sl]
        kk = kk / jnp.maximum(jnp.sqrt(_dot_split(kk * kk, head_ones)), 1e-12)
        k = k * (1.0 + (asig - 1.0) * ka_ref[:, sl])
        a = -kk
        b = kk * asig

        cum = _cumsum_rows(lw, tri_incl)
        decay_to = jnp.exp(cum)
        decay_end = decay_to[L - 1:L, :]
        inv = jnp.exp(-cum)
        a_x = _pair_rows(a * jnp.exp(cum - lw), lane_lo)
        r_x = _pair_rows(r * decay_to, lane_lo)
        b_x = _pair_rows(b * inv, lane_lo)
        k_x = _pair_rows(k * inv, lane_lo)
        v_x = _pair_rows(v, lane_lo)

        ar = jnp.concatenate([a_x, r_x], axis=0)
        bk = jnp.concatenate([b_x, k_x], axis=0)
        scores = _dot_nt(ar, bk)
        a_ab = jnp.where(strict_lower, scores[:L2, :L2], 0.0)
        a_ak = jnp.where(strict_lower, scores[:L2, L2:], 0.0)
        r_b = jnp.where(lower, scores[L2:, :L2], 0.0)
        r_k = jnp.where(lower, scores[L2:, L2:], 0.0)

        s0 = state_ref[p]
        from_state = _dot_nt(ar, s0)
        u = from_state[:L2] + _dot(a_ak, v_x)
        apow = a_ab
        for step in range(int(math.log2(L))):
            u = u + _dot(apow, u)
            if step + 1 < int(math.log2(L)):
                apow = _dot(apow, apow)
        uv = jnp.concatenate([u, v_x], axis=0)
        y_x = from_state[L2:] + _dot(jnp.concatenate([r_b, r_k], axis=1), uv)
        bk_end = bk * decay_end
        state_ref[p] = s0 * decay_end + _dot_tn(uv, bk_end)
        y = y_x[:L] + y_x[L:]

        mu = _dot_split(y, head_ones) * (1.0 / HEAD)
        yc = y - mu
        var = _dot_split(yc * yc, head_ones) * (1.0 / HEAD)
        yn = yc * lax.rsqrt(var + GN_EPS) * lnw_ref[:, sl] + lnb_ref[:, sl]
        bonus = _dot_split(r * k * rk_ref[:, sl], head_ones) * v
        y_ref[0, :, sl] = yn + bonus


def _wkv(r, lw, k, v, asig, k_k, k_a, r_k, lnx_w, lnx_b):
    B, T, D = r.shape
    L = WKV_CHUNK
    width = WKV_PAIRS * LANES
    seq = pl.BlockSpec((1, L, width), lambda b, h, c: (b, c, h))
    vec = pl.BlockSpec((1, width), lambda b, h, c: (0, h))
    row = lambda a: a.reshape(1, D)
    return pl.pallas_call(
        _wkv_kernel,
        grid=(B, D // width, T // L),
        in_specs=[seq] * 5 + [vec] * 5,
        out_specs=seq,
        out_shape=jax.ShapeDtypeStruct((B, T, D), F32),
        scratch_shapes=[pltpu.VMEM((WKV_PAIRS, LANES, LANES), F32)],
        compiler_params=_params("parallel", "parallel", "arbitrary"),
        name="wkv7_chunked",
    )(r, lw, k, v, asig, row(k_k), row(k_a), row(r_k), row(lnx_w), row(lnx_b))


def _out_proj_kernel(has_gate, *refs):
    if has_gate:
        y_ref, gate_ref, x_ref, w_ref, b_ref, g_ref, o_ref = refs
        y = y_ref[...] * gate_ref[...]
    else:
        y_ref, x_ref, w_ref, b_ref, g_ref, o_ref = refs
        y = y_ref[...]
    out = _dot(y, w_ref[...]) + b_ref[...]
    o_ref[...] = x_ref[...] + _rms(out, g_ref[...])


def _out_proj(y, gate, x, w, bias, g):
    M, D = x.shape
    tm = FFN_TILE
    tile = pl.BlockSpec((tm, D), lambda i: (i, 0))
    acts = [y] + ([gate] if gate is not None else []) + [x]
    consts = [w.astype(BF16), bias.reshape(1, D), g.reshape(1, D)]
    return pl.pallas_call(
        functools.partial(_out_proj_kernel, gate is not None),
        grid=(M // tm,),
        in_specs=[tile] * len(acts) + [_const_spec(c.shape) for c in consts],
        out_specs=tile,
        out_shape=jax.ShapeDtypeStruct((M, D), F32),
        compiler_params=_params("parallel"),
        name="out_proj",
    )(*acts, *consts)


def _ffn_kernel(x_ref, gin_ref, wgu_ref, wd_ref, gout_ref, o_ref):
    x = x_ref[...]
    h = _rms(x, gin_ref[...]).astype(BF16)
    acc = jnp.zeros(x.shape, F32)
    for j in range(D_FF // FFN_CHUNK):
        lo = j * FFN_CHUNK
        gate = jnp.dot(h, wgu_ref[:, lo:lo + FFN_CHUNK], preferred_element_type=F32)
        up = jnp.dot(h, wgu_ref[:, D_FF + lo:D_FF + lo + FFN_CHUNK], preferred_element_type=F32)
        act = gate * _sigmoid(gate) * up
        acc = acc + _dot(act, wd_ref[lo:lo + FFN_CHUNK, :])
    o_ref[...] = x + _rms(acc, gout_ref[...])


def _ffn(x, g_in, w_gate_up, w_down, g_out):
    M, D = x.shape
    tm = FFN_TILE
    tile = pl.BlockSpec((tm, D), lambda i: (i, 0))
    consts = [g_in.reshape(1, D), w_gate_up.astype(BF16), w_down.astype(BF16), g_out.reshape(1, D)]
    return pl.pallas_call(
        _ffn_kernel,
        grid=(M // tm,),
        in_specs=[tile] + [_const_spec(c.shape) for c in consts],
        out_specs=tile,
        out_shape=jax.ShapeDtypeStruct((M, D), F32),
        compiler_params=_params("parallel"),
        name="swiglu_ffn",
    )(x, *consts)


def _rope_tables(T):
    half = ROPE_DIM // 2
    pos = jnp.arange(T, dtype=F32)
    inv_freq = ROPE_THETA ** (-jnp.arange(0, ROPE_DIM, 2, dtype=F32) / ROPE_DIM)
    ang = pos[:, None] * inv_freq[None, :]
    cos, sin = jnp.cos(ang), jnp.sin(ang)
    rest = HEAD - ROPE_DIM
    c = jnp.concatenate([cos, cos, jnp.ones((T, rest), F32)], axis=1)
    s_next = jnp.concatenate([-sin, jnp.zeros((T, half + rest), F32)], axis=1)
    s_prev = jnp.concatenate([jnp.zeros((T, half), F32), sin, jnp.zeros((T, rest), F32)], axis=1)
    return tuple(jnp.tile(t, (1, PAIR)) for t in (c, s_next, s_prev))


def _qkv_kernel(x_ref, g_ref, w_ref, b_ref, c_ref, sn_ref, sp_ref, o_ref):
    half = ROPE_DIM // 2
    qk_dim = D_MODEL + KV_DIM
    reps = qk_dim // LANES
    h = _rms(x_ref[...], g_ref[...])
    qkv = _dot(h, w_ref[...]) + b_ref[...]
    qk = qkv[:, :qk_dim]
    c = jnp.concatenate([c_ref[...]] * reps, axis=1)
    s_next = jnp.concatenate([sn_ref[...]] * reps, axis=1)
    s_prev = jnp.concatenate([sp_ref[...]] * reps, axis=1)
    rot = (qk * c + pltpu.roll(qk, qk_dim - half, 1) * s_next + pltpu.roll(qk, half, 1) * s_prev)
    o_ref[:, :qk_dim] = rot
    o_ref[:, qk_dim:] = qkv[:, qk_dim:]


def _qkv_proj(x, g, w, b, tables, T):
    M, D = x.shape
    tm = FFN_TILE
    tiles_per_seq = T // tm
    tile = pl.BlockSpec((tm, D), lambda i: (i, 0))
    tab = pl.BlockSpec((tm, LANES), lambda i: (i % tiles_per_seq, 0))
    consts = [g.reshape(1, D), w.astype(BF16), b.reshape(1, QKV_DIM)]
    return pl.pallas_call(
        _qkv_kernel,
        grid=(M // tm,),
        in_specs=[tile] + [_const_spec(c.shape) for c in consts] + [tab] * 3,
        out_specs=pl.BlockSpec((tm, QKV_DIM), lambda i: (i, 0)),
        out_shape=jax.ShapeDtypeStruct((M, QKV_DIM), F32),
        compiler_params=_params("parallel"),
        name="attn_qkv_rope",
    )(x, *consts, *tables)


def _attn_kernel(sinks_ref, q_ref, kp_ref, kc_ref, vp_ref, vc_ref, x_ref, w_ref, b_ref, g_ref,
                 o_ref):
    n = pl.program_id(1)
    qi = lax.broadcasted_iota(jnp.int32, (BLOCK, 2 * BLOCK), 0)
    kj = lax.broadcasted_iota(jnp.int32, (BLOCK, 2 * BLOCK), 1)
    rel = qi + BLOCK - kj
    valid = (rel >= 0) & (rel < WINDOW) & ((n > 0) | (kj >= BLOCK))
    q = q_ref[0] * (HEAD ** -0.5)
    keys = jnp.concatenate([kp_ref[0], kc_ref[0]], axis=0)
    vals = jnp.concatenate([vp_ref[0], vc_ref[0]], axis=0)
    outs = []
    for hk in range(N_KV_HEADS):
        kh = keys[:, hk * HEAD:(hk + 1) * HEAD]
        vh = vals[:, hk * HEAD:(hk + 1) * HEAD]
        for gi in range(GROUP):
            hq = hk * GROUP + gi
            s = _dot_nt(q[:, hq * HEAD:(hq + 1) * HEAD], kh)
            sink = sinks_ref[hq]
            m = jnp.maximum(jnp.max(jnp.where(valid, s, -jnp.inf), axis=-1, keepdims=True), sink)
            p = jnp.where(valid, jnp.exp(s - m), 0.0)
            denom = jnp.sum(p, axis=-1, keepdims=True) + jnp.exp(sink - m)
            outs.append(_dot(p, vh) / denom)
    o = jnp.concatenate(outs, axis=1)
    out = _dot(o, w_ref[...]) + b_ref[...]
    o_ref[0] = x_ref[0] + _rms(out, g_ref[...])


def _attention(qkv, x, sinks, w_o, b_o, g):
    B, T, D = x.shape
    nb = T // BLOCK
    k_blk = D_MODEL // KV_DIM
    v_blk = k_blk + 1
    cur = lambda b, n: (b, n, 0)
    consts = [w_o.astype(BF16), b_o.reshape(1, D), g.reshape(1, D)]
    return pl.pallas_call(
        _attn_kernel,
        grid=(B, nb),
        in_specs=[
            pl.BlockSpec(memory_space=pltpu.SMEM),
            pl.BlockSpec((1, BLOCK, D), cur),
            pl.BlockSpec((1, BLOCK, KV_DIM), lambda b, n: (b, jnp.maximum(n - 1, 0), k_blk)),
            pl.BlockSpec((1, BLOCK, KV_DIM), lambda b, n: (b, n, k_blk)),
            pl.BlockSpec((1, BLOCK, KV_DIM), lambda b, n: (b, jnp.maximum(n - 1, 0), v_blk)),
            pl.BlockSpec((1, BLOCK, KV_DIM), lambda b, n: (b, n, v_blk)),
            pl.BlockSpec((1, BLOCK, D), cur),
        ] + [_const_spec(c.shape) for c in consts],
        out_specs=pl.BlockSpec((1, BLOCK, D), cur),
        out_shape=jax.ShapeDtypeStruct((B, T, D), F32),
        compiler_params=_params("parallel", "arbitrary"),
        name="swa_sink_attention",
    )(sinks, qkv, qkv, qkv, qkv, qkv, x, *consts)


def kernel(x, norm_g, rwkv_x_mix, rwkv_w_rkv, rwkv_w0, rwkv_w1, rwkv_w2, rwkv_a0, rwkv_a1,
           rwkv_a2, rwkv_v0, rwkv_v1, rwkv_v2, rwkv_g1, rwkv_g2, rwkv_k_k, rwkv_k_a, rwkv_r_k,
           rwkv_lnx_w, rwkv_lnx_b, rwkv_w_o, attn_w_qkv, attn_b_qkv, attn_sinks, attn_w_o,
           attn_b_o, ffn_w_gate_up, ffn_w_down):
    B, T, D = x.shape
    M = B * T
    depth = norm_g.shape[0]
    tables = _rope_tables(T)
    zero_bias = jnp.zeros((D,), F32)
    v_first = None
    for i in range(depth):
        j = i // 2
        if i % 2 == 0:
            vres = None if j == 0 else (v_first, rwkv_v0[j - 1], rwkv_v1[j - 1], rwkv_v2[j - 1])
            r, lw, k, v, asig, gate = _rwkv_proj(
                x, norm_g[i, 0], rwkv_x_mix[j], rwkv_w_rkv[j], rwkv_w0[j], rwkv_w1[j], rwkv_w2[j],
                rwkv_a0[j], rwkv_a1[j], rwkv_a2[j], rwkv_g1[j], rwkv_g2[j], vres)
            if j == 0:
                v_first = v
            y = _wkv(r, lw, k, v, asig, rwkv_k_k[j], rwkv_k_a[j], rwkv_r_k[j].reshape(D),
                     rwkv_lnx_w[j], rwkv_lnx_b[j])
            x = _out_proj(y.reshape(M, D), gate.reshape(M, D), x.reshape(M, D), rwkv_w_o[j],
                          zero_bias, norm_g[i, 1])
        else:
            qkv = _qkv_proj(x.reshape(M, D), norm_g[i, 0], attn_w_qkv[j], attn_b_qkv[j], tables, T)
            x = _attention(qkv.reshape(B, T, QKV_DIM), x.reshape(B, T, D), attn_sinks[j],
                           attn_w_o[j], attn_b_o[j], norm_g[i, 1]).reshape(M, D)
        x = _ffn(x, norm_g[i, 2], ffn_w_gate_up[i], ffn_w_down[i], norm_g[i, 3]).reshape(B, T, D)
    return x
```

```python
import functools
import math

import jax
import jax.numpy as jnp
from jax import lax
from jax.experimental import pallas as pl
from jax.experimental.pallas import tpu as pltpu

D_MODEL = 1024
HEAD = 64
N_HEADS = D_MODEL // HEAD
N_KV_HEADS = 4
GROUP = N_HEADS // N_KV_HEADS
KV_DIM = N_KV_HEADS * HEAD
QKV_DIM = D_MODEL + 2 * KV_DIM
WINDOW = 128
BLOCK = 128
ROPE_THETA = 500000.0
ROPE_DIM = HEAD // 4
D_FF = 2816
RMS_EPS = 1e-6
GN_EPS = 64e-5

LANES = 128
PAIR = LANES // HEAD
WKV_CHUNK = 64
WKV_GROUP_W = 256
WKV_BATCH = 2
TOKEN_TILE = 256
FFN_TILE = 512
FFN_CHUNK = 1408
VMEM_LIMIT = 56 * 1024 * 1024

BF16 = jnp.bfloat16
F32 = jnp.float32

assert WKV_CHUNK == HEAD


def _dot(a, b):
    return jnp.dot(a.astype(BF16), b.astype(BF16), preferred_element_type=F32)


def _dot_nt(a, b):
    return lax.dot_general(a.astype(BF16), b.astype(BF16), (((1,), (1,)), ((), ())),
                           preferred_element_type=F32)


def _dot_tn(a, b):
    return lax.dot_general(a.astype(BF16), b.astype(BF16), (((0,), (0,)), ((), ())),
                           preferred_element_type=F32)


def _cumsum_rows(x, tri01):
    hi = x.astype(BF16)
    lo = (x - hi.astype(F32)).astype(BF16)
    return (jnp.dot(tri01, hi, preferred_element_type=F32)
            + jnp.dot(tri01, lo, preferred_element_type=F32))


def _rms(x, g):
    return x * lax.rsqrt(jnp.mean(x * x, axis=-1, keepdims=True) + RMS_EPS) * g


def _sigmoid(x):
    return 1.0 / (1.0 + jnp.exp(-x))


def _const_spec(shape):
    zeros = (0,) * len(shape)
    return pl.BlockSpec(shape, lambda *_: zeros, pipeline_mode=pl.Buffered(1))


def _params(*sem):
    return pltpu.CompilerParams(dimension_semantics=sem, vmem_limit_bytes=VMEM_LIMIT)


def _rwkv_proj_kernel(has_vres, *refs):
    if has_vres:
        (x_ref, halo_ref, g_ref, mix_ref, wrkv_ref, w0_ref, w1_ref, w2_ref, a0_ref, a1_ref,
         a2_ref, g1_ref, g2_ref, vf_ref, v0_ref, v1_ref, v2_ref,
         r_ref, lw_ref, k_ref, v_ref, as_ref, gate_ref) = refs
    else:
        (x_ref, halo_ref, g_ref, mix_ref, wrkv_ref, w0_ref, w1_ref, w2_ref, a0_ref, a1_ref,
         a2_ref, g1_ref, g2_ref,
         r_ref, lw_ref, k_ref, v_ref, as_ref, gate_ref) = refs
    i = pl.program_id(1)
    g = g_ref[...]
    h = _rms(x_ref[0], g)
    prev_last = _rms(halo_ref[0][7:8, :], g)
    prev_last = jnp.where(i == 0, 0.0, prev_last)
    row = lax.broadcasted_iota(jnp.int32, h.shape, 0)
    h_prev = jnp.where(row == 0, prev_last, pltpu.roll(h, 1, 0))
    xx = h_prev - h
    mix = mix_ref[...]
    xr, xk, xv, xw, xa, xg = (h + xx * mix[j:j + 1, :] for j in range(6))

    r_ref[0] = _dot(xr, wrkv_ref[0])
    k_ref[0] = _dot(xk, wrkv_ref[1])
    v = _dot(xv, wrkv_ref[2])
    if has_vres:
        mixv = _sigmoid(v0_ref[...] + _dot(_dot(xv, v1_ref[...]), v2_ref[...]))
        v = v + (vf_ref[0] - v) * mixv
    v_ref[0] = v
    z = w0_ref[...] + _dot(jnp.tanh(_dot(xw, w1_ref[...])), w2_ref[...])
    lw_ref[0] = (-math.exp(-0.5)) * _sigmoid(z)
    as_ref[0] = _sigmoid(a0_ref[...] + _dot(_dot(xa, a1_ref[...]), a2_ref[...]))
    gate_ref[0] = _dot(_sigmoid(_dot(xg, g1_ref[...])), g2_ref[...])


def _rwkv_proj(x, g, mix, wrkv, w0, w1, w2, a0, a1, a2, g1, g2, vres):
    B, T, D = x.shape
    tm = TOKEN_TILE
    tile = pl.BlockSpec((1, tm, D), lambda b, i: (b, i, 0))
    halo = pl.BlockSpec((1, 8, D), lambda b, i: (b, jnp.maximum(i * (tm // 8) - 1, 0), 0))
    row = lambda a: a.reshape(1, -1)
    args = [x, x, row(g), mix, wrkv.astype(BF16), row(w0), w1.astype(BF16), w2.astype(BF16),
            row(a0), a1.astype(BF16), a2.astype(BF16), g1.astype(BF16), g2.astype(BF16)]
    specs = [tile, halo] + [_const_spec(a.shape) for a in args[2:]]
    if vres is not None:
        v_first, v0, v1, v2 = vres
        extra = [row(v0), v1.astype(BF16), v2.astype(BF16)]
        args += [v_first] + extra
        specs += [tile] + [_const_spec(a.shape) for a in extra]
    out = jax.ShapeDtypeStruct((B, T, D), F32)
    return pl.pallas_call(
        functools.partial(_rwkv_proj_kernel, vres is not None),
        grid=(B, T // tm),
        in_specs=specs,
        out_specs=[tile] * 6,
        out_shape=[out] * 6,
        compiler_params=_params("parallel", "arbitrary"),
        name="rwkv_proj",
    )(*args)


def _wkv_kernel(r_ref, lw_ref, k_ref, v_ref, as_ref, kk_ref, ka_ref, rk_ref, lnw_ref, lnb_ref,
                y_ref, state_ref):
    L = WKV_CHUNK
    W = WKV_GROUP_W
    groups = D_MODEL // W
    chains = [(bi, q) for bi in range(WKV_BATCH) for q in range(groups)]
    n = len(chains)
    n_double = int(math.log2(L))

    @pl.when(pl.program_id(1) == 0)
    def _():
        state_ref[...] = jnp.zeros_like(state_ref)

    ti = lax.broadcasted_iota(jnp.int32, (L, L), 0)
    tj = lax.broadcasted_iota(jnp.int32, (L, L), 1)
    tri_incl = (ti >= tj).astype(BF16)
    step_row = lax.broadcasted_iota(jnp.int32, (L, W), 0)
    step_col = lax.broadcasted_iota(jnp.int32, (L, W), 1) % L
    strict_lower = step_row > step_col
    lower = step_row >= step_col
    head_i = lax.broadcasted_iota(jnp.int32, (W, W), 0) // HEAD
    head_j = lax.broadcasted_iota(jnp.int32, (W, W), 1) // HEAD
    same_head = head_i == head_j
    head_ones = same_head.astype(BF16)

    def blockdiag(x):
        xb = x.astype(BF16)
        return jnp.where(same_head, jnp.concatenate([xb] * (W // L), axis=0), jnp.zeros((), BF16))

    def headsum(xs):
        out = jnp.dot(jnp.concatenate(xs, axis=0).astype(BF16), head_ones,
                      preferred_element_type=F32)
        return [out[i * L:(i + 1) * L] for i in range(len(xs))]

    def load(ref, c):
        return ref[c[0], :, c[1] * W:(c[1] + 1) * W]

    def vec(ref, c):
        return ref[:, c[1] * W:(c[1] + 1) * W]

    r = [load(r_ref, c) for c in chains]
    lw = [load(lw_ref, c) for c in chains]
    k_in = [load(k_ref, c) for c in chains]
    v = [load(v_ref, c) for c in chains]
    asig = [load(as_ref, c) for c in chains]

    kk = [k_in[i] * vec(kk_ref, c) for i, c in enumerate(chains)]
    norm2 = headsum([x * x for x in kk])
    kk = [x / jnp.maximum(jnp.sqrt(s), 1e-12) for x, s in zip(kk, norm2)]
    k = [k_in[i] * (1.0 + (asig[i] - 1.0) * vec(ka_ref, c)) for i, c in enumerate(chains)]

    cum = [_cumsum_rows(x, tri_incl) for x in lw]
    decay_to = [jnp.exp(x) for x in cum]
    decay_end = [x[L - 1:L, :] for x in decay_to]
    inv = [jnp.exp(-x) for x in cum]
    ar = [jnp.concatenate([-kk[i] * jnp.exp(cum[i] - lw[i]), r[i] * decay_to[i]],
                          axis=0).astype(BF16) for i in range(n)]
    b_s = [kk[i] * asig[i] * inv[i] for i in range(n)]
    k_s = [k[i] * inv[i] for i in range(n)]
    v_bd = [blockdiag(x) for x in v]

    s_b = [_dot_nt(ar[i], blockdiag(b_s[i])) for i in range(n)]
    s_k = [_dot_nt(ar[i], blockdiag(k_s[i])) for i in range(n)]
    s0 = [state_ref[i] for i in range(n)]
    from_state = [_dot_nt(ar[i], s0[i]) for i in range(n)]

    apow = [jnp.where(strict_lower, x[:L], 0.0) for x in s_b]
    u = [from_state[i][:L] + _dot(jnp.where(strict_lower, s_k[i][:L], 0.0), v_bd[i])
         for i in range(n)]
    for step in range(n_double):
        u = [u[i] + _dot(apow[i], blockdiag(u[i])) for i in range(n)]
        if step + 1 < n_double:
            apow = [_dot(x, blockdiag(x)) for x in apow]

    y = []
    for i in range(n):
        lhs = jnp.concatenate([jnp.where(lower, s_b[i][L:], 0.0),
                               jnp.where(lower, s_k[i][L:], 0.0)], axis=1)
        rhs = jnp.concatenate([blockdiag(u[i]), v_bd[i]], axis=0)
        y.append(from_state[i][L:] + _dot(lhs, rhs))
    for i in range(n):
        uv = jnp.concatenate([u[i], v[i]], axis=0)
        bk_end = jnp.concatenate([b_s[i], k_s[i]], axis=0) * decay_end[i]
        state_ref[i] = s0[i] * decay_end[i] + jnp.where(same_head, _dot_tn(uv, bk_end), 0.0)

    mu = headsum(y)
    yc = [y[i] - mu[i] * (1.0 / HEAD) for i in range(n)]
    var = headsum([x * x for x in yc])
    rk = headsum([r[i] * k[i] * vec(rk_ref, c) for i, c in enumerate(chains)])
    for i, c in enumerate(chains):
        yn = yc[i] * lax.rsqrt(var[i] * (1.0 / HEAD) + GN_EPS) * vec(lnw_ref, c) + vec(lnb_ref, c)
        y_ref[c[0], :, c[1] * W:(c[1] + 1) * W] = yn + rk[i] * v[i]


def _wkv(r, lw, k, v, asig, k_k, k_a, r_k, lnx_w, lnx_b):
    B, T, D = r.shape
    L = WKV_CHUNK
    seq = pl.BlockSpec((WKV_BATCH, L, D), lambda b, c: (b, c, 0))
    vec = pl.BlockSpec((1, D), lambda b, c: (0, 0))
    row = lambda a: a.reshape(1, D)
    n_chains = WKV_BATCH * (D // WKV_GROUP_W)
    return pl.pallas_call(
        _wkv_kernel,
        grid=(B // WKV_BATCH, T // L),
        in_specs=[seq] * 5 + [vec] * 5,
        out_specs=seq,
        out_shape=jax.ShapeDtypeStruct((B, T, D), F32),
        scratch_shapes=[pltpu.VMEM((n_chains, WKV_GROUP_W, WKV_GROUP_W), F32)],
        compiler_params=_params("parallel", "arbitrary"),
        name="wkv7_chunked",
    )(r, lw, k, v, asig, row(k_k), row(k_a), row(r_k), row(lnx_w), row(lnx_b))


def _out_proj_kernel(has_gate, *refs):
    if has_gate:
        y_ref, gate_ref, x_ref, w_ref, b_ref, g_ref, o_ref = refs
        y = y_ref[...] * gate_ref[...]
    else:
        y_ref, x_ref, w_ref, b_ref, g_ref, o_ref = refs
        y = y_ref[...]
    out = _dot(y, w_ref[...]) + b_ref[...]
    o_ref[...] = x_ref[...] + _rms(out, g_ref[...])


def _out_proj(y, gate, x, w, bias, g):
    M, D = x.shape
    tm = FFN_TILE
    tile = pl.BlockSpec((tm, D), lambda i: (i, 0))
    acts = [y] + ([gate] if gate is not None else []) + [x]
    consts = [w.astype(BF16), bias.reshape(1, D), g.reshape(1, D)]
    return pl.pallas_call(
        functools.partial(_out_proj_kernel, gate is not None),
        grid=(M // tm,),
        in_specs=[tile] * len(acts) + [_const_spec(c.shape) for c in consts],
        out_specs=tile,
        out_shape=jax.ShapeDtypeStruct((M, D), F32),
        compiler_params=_params("parallel"),
        name="out_proj",
    )(*acts, *consts)


def _ffn_kernel(x_ref, gin_ref, wgu_ref, wd_ref, gout_ref, o_ref):
    x = x_ref[...]
    h = _rms(x, gin_ref[...]).astype(BF16)
    acc = jnp.zeros(x.shape, F32)
    for j in range(D_FF // FFN_CHUNK):
        lo = j * FFN_CHUNK
        gate = jnp.dot(h, wgu_ref[:, lo:lo + FFN_CHUNK], preferred_element_type=F32)
        up = jnp.dot(h, wgu_ref[:, D_FF + lo:D_FF + lo + FFN_CHUNK], preferred_element_type=F32)
        act = gate * _sigmoid(gate) * up
        acc = acc + _dot(act, wd_ref[lo:lo + FFN_CHUNK, :])
    o_ref[...] = x + _rms(acc, gout_ref[...])


def _ffn(x, g_in, w_gate_up, w_down, g_out):
    M, D = x.shape
    tm = FFN_TILE
    tile = pl.BlockSpec((tm, D), lambda i: (i, 0))
    consts = [g_in.reshape(1, D), w_gate_up.astype(BF16), w_down.astype(BF16), g_out.reshape(1, D)]
    return pl.pallas_call(
        _ffn_kernel,
        grid=(M // tm,),
        in_specs=[tile] + [_const_spec(c.shape) for c in consts],
        out_specs=tile,
        out_shape=jax.ShapeDtypeStruct((M, D), F32),
        compiler_params=_params("parallel"),
        name="swiglu_ffn",
    )(x, *consts)


def _rope_tables(T):
    half = ROPE_DIM // 2
    pos = jnp.arange(T, dtype=F32)
    inv_freq = ROPE_THETA ** (-jnp.arange(0, ROPE_DIM, 2, dtype=F32) / ROPE_DIM)
    ang = pos[:, None] * inv_freq[None, :]
    cos, sin = jnp.cos(ang), jnp.sin(ang)
    rest = HEAD - ROPE_DIM
    c = jnp.concatenate([cos, cos, jnp.ones((T, rest), F32)], axis=1)
    s_next = jnp.concatenate([-sin, jnp.zeros((T, half + rest), F32)], axis=1)
    s_prev = jnp.concatenate([jnp.zeros((T, half), F32), sin, jnp.zeros((T, rest), F32)], axis=1)
    return tuple(jnp.tile(t, (1, PAIR)) for t in (c, s_next, s_prev))


def _qkv_kernel(x_ref, g_ref, w_ref, b_ref, c_ref, sn_ref, sp_ref, o_ref):
    half = ROPE_DIM // 2
    qk_dim = D_MODEL + KV_DIM
    reps = qk_dim // LANES
    h = _rms(x_ref[...], g_ref[...])
    qkv = _dot(h, w_ref[...]) + b_ref[...]
    qk = qkv[:, :qk_dim]
    c = jnp.concatenate([c_ref[...]] * reps, axis=1)
    s_next = jnp.concatenate([sn_ref[...]] * reps, axis=1)
    s_prev = jnp.concatenate([sp_ref[...]] * reps, axis=1)
    rot = (qk * c + pltpu.roll(qk, qk_dim - half, 1) * s_next + pltpu.roll(qk, half, 1) * s_prev)
    o_ref[:, :qk_dim] = rot
    o_ref[:, qk_dim:] = qkv[:, qk_dim:]


def _qkv_proj(x, g, w, b, tables, T):
    M, D = x.shape
    tm = FFN_TILE
    tiles_per_seq = T // tm
    tile = pl.BlockSpec((tm, D), lambda i: (i, 0))
    tab = pl.BlockSpec((tm, LANES), lambda i: (i % tiles_per_seq, 0))
    consts = [g.reshape(1, D), w.astype(BF16), b.reshape(1, QKV_DIM)]
    return pl.pallas_call(
        _qkv_kernel,
        grid=(M // tm,),
        in_specs=[tile] + [_const_spec(c.shape) for c in consts] + [tab] * 3,
        out_specs=pl.BlockSpec((tm, QKV_DIM), lambda i: (i, 0)),
        out_shape=jax.ShapeDtypeStruct((M, QKV_DIM), F32),
        compiler_params=_params("parallel"),
        name="attn_qkv_rope",
    )(x, *consts, *tables)


def _attn_kernel(sinks_ref, q_ref, kp_ref, kc_ref, vp_ref, vc_ref, x_ref, w_ref, b_ref, g_ref,
                 o_ref):
    n = pl.program_id(1)
    qi = lax.broadcasted_iota(jnp.int32, (BLOCK, 2 * BLOCK), 0)
    kj = lax.broadcasted_iota(jnp.int32, (BLOCK, 2 * BLOCK), 1)
    rel = qi + BLOCK - kj
    valid = (rel >= 0) & (rel < WINDOW) & ((n > 0) | (kj >= BLOCK))
    q = q_ref[0] * (HEAD ** -0.5)
    keys = jnp.concatenate([kp_ref[0], kc_ref[0]], axis=0)
    vals = jnp.concatenate([vp_ref[0], vc_ref[0]], axis=0)
    outs = []
    for hk in range(N_KV_HEADS):
        kh = keys[:, hk * HEAD:(hk + 1) * HEAD]
        vh = vals[:, hk * HEAD:(hk + 1) * HEAD]
        for gi in range(GROUP):
            hq = hk * GROUP + gi
            s = _dot_nt(q[:, hq * HEAD:(hq + 1) * HEAD], kh)
            sink = sinks_ref[hq]
            m = jnp.maximum(jnp.max(jnp.where(valid, s, -jnp.inf), axis=-1, keepdims=True), sink)
            p = jnp.where(valid, jnp.exp(s - m), 0.0)
            denom = jnp.sum(p, axis=-1, keepdims=True) + jnp.exp(sink - m)
            outs.append(_dot(p, vh) / denom)
    o = jnp.concatenate(outs, axis=1)
    out = _dot(o, w_ref[...]) + b_ref[...]
    o_ref[0] = x_ref[0] + _rms(out, g_ref[...])


def _attention(qkv, x, sinks, w_o, b_o, g):
    B, T, D = x.shape
    nb = T // BLOCK
    k_blk = D_MODEL // KV_DIM
    v_blk = k_blk + 1
    cur = lambda b, n: (b, n, 0)
    consts = [w_o.astype(BF16), b_o.reshape(1, D), g.reshape(1, D)]
    return pl.pallas_call(
        _attn_kernel,
        grid=(B, nb),
        in_specs=[
            pl.BlockSpec(memory_space=pltpu.SMEM),
            pl.BlockSpec((1, BLOCK, D), cur),
            pl.BlockSpec((1, BLOCK, KV_DIM), lambda b, n: (b, jnp.maximum(n - 1, 0), k_blk)),
            pl.BlockSpec((1, BLOCK, KV_DIM), lambda b, n: (b, n, k_blk)),
            pl.BlockSpec((1, BLOCK, KV_DIM), lambda b, n: (b, jnp.maximum(n - 1, 0), v_blk)),
            pl.BlockSpec((1, BLOCK, KV_DIM), lambda b, n: (b, n, v_blk)),
            pl.BlockSpec((1, BLOCK, D), cur),
        ] + [_const_spec(c.shape) for c in consts],
        out_specs=pl.BlockSpec((1, BLOCK, D), cur),
        out_shape=jax.ShapeDtypeStruct((B, T, D), F32),
        compiler_params=_params("parallel", "arbitrary"),
        name="swa_sink_attention",
    )(sinks, qkv, qkv, qkv, qkv, qkv, x, *consts)


def kernel(x, norm_g, rwkv_x_mix, rwkv_w_rkv, rwkv_w0, rwkv_w1, rwkv_w2, rwkv_a0, rwkv_a1,
           rwkv_a2, rwkv_v0, rwkv_v1, rwkv_v2, rwkv_g1, rwkv_g2, rwkv_k_k, rwkv_k_a, rwkv_r_k,
           rwkv_lnx_w, rwkv_lnx_b, rwkv_w_o, attn_w_qkv, attn_b_qkv, attn_sinks, attn_w_o,
           attn_b_o, ffn_w_gate_up, ffn_w_down):
    B, T, D = x.shape
    M = B * T
    depth = norm_g.shape[0]
    tables = _rope_tables(T)
    zero_bias = jnp.zeros((D,), F32)
    v_first = None
    for i in range(depth):
        j = i // 2
        if i % 2 == 0:
            vres = None if j == 0 else (v_first, rwkv_v0[j - 1], rwkv_v1[j - 1], rwkv_v2[j - 1])
            r, lw, k, v, asig, gate = _rwkv_proj(
                x, norm_g[i, 0], rwkv_x_mix[j], rwkv_w_rkv[j], rwkv_w0[j], rwkv_w1[j], rwkv_w2[j],
                rwkv_a0[j], rwkv_a1[j], rwkv_a2[j], rwkv_g1[j], rwkv_g2[j], vres)
            if j == 0:
                v_first = v
            y = _wkv(r, lw, k, v, asig, rwkv_k_k[j], rwkv_k_a[j], rwkv_r_k[j].reshape(D),
                     rwkv_lnx_w[j], rwkv_lnx_b[j])
            x = _out_proj(y.reshape(M, D), gate.reshape(M, D), x.reshape(M, D), rwkv_w_o[j],
                          zero_bias, norm_g[i, 1])
        else:
            qkv = _qkv_proj(x.reshape(M, D), norm_g[i, 0], attn_w_qkv[j], attn_b_qkv[j], tables, T)
            x = _attention(qkv.reshape(B, T, QKV_DIM), x.reshape(B, T, D), attn_sinks[j],
                           attn_w_o[j], attn_b_o[j], norm_g[i, 1]).reshape(M, D)
        x = _ffn(x, norm_g[i, 2], ffn_w_gate_up[i], ffn_w_down[i], norm_g[i, 3]).reshape(B, T, D)
    return x
```

```python
import functools
import math

import jax
import jax.numpy as jnp
from jax import lax
from jax.experimental import pallas as pl
from jax.experimental.pallas import tpu as pltpu

D_MODEL = 1024
HEAD = 64
N_HEADS = D_MODEL // HEAD
N_KV_HEADS = 4
GROUP = N_HEADS // N_KV_HEADS
KV_DIM = N_KV_HEADS * HEAD
QKV_DIM = D_MODEL + 2 * KV_DIM
WINDOW = 128
BLOCK = 128
ROPE_THETA = 500000.0
ROPE_DIM = HEAD // 4
D_FF = 2816
RMS_EPS = 1e-6
GN_EPS = 64e-5
LOG2E = math.log2(math.e)
ATTN_Q_SCALE = HEAD ** -0.5 * LOG2E

LANES = 128
PAIR = LANES // HEAD
WKV_CHUNK = 64
WKV_GROUP_W = 256
WKV_BATCH = 2
ATTN_BLOCKS = 2
ATTN_OVERLAP = 8
TOKEN_TILE = 256
FFN_TILE = 512
FFN_CHUNK = 512
VMEM_LIMIT = 56 * 1024 * 1024

BF16 = jnp.bfloat16
F32 = jnp.float32

assert WKV_CHUNK == HEAD
assert WINDOW <= BLOCK


def _dot(a, b):
    return jnp.dot(a.astype(BF16), b.astype(BF16), preferred_element_type=F32)


def _dot_nt(a, b):
    return lax.dot_general(a.astype(BF16), b.astype(BF16), (((1,), (1,)), ((), ())),
                           preferred_element_type=F32)


def _dot_tn(a, b):
    return lax.dot_general(a.astype(BF16), b.astype(BF16), (((0,), (0,)), ((), ())),
                           preferred_element_type=F32)


def _cumsum_rows(x, tri01):
    hi = x.astype(BF16)
    lo = (x - hi.astype(F32)).astype(BF16)
    return (jnp.dot(tri01, hi, preferred_element_type=F32)
            + jnp.dot(tri01, lo, preferred_element_type=F32))


def _rms(x, g):
    return x * lax.rsqrt(jnp.mean(x * x, axis=-1, keepdims=True) + RMS_EPS) * g


def _sigmoid(x):
    return 1.0 / (1.0 + jnp.exp(-x))


def _const_spec(shape):
    zeros = (0,) * len(shape)
    return pl.BlockSpec(shape, lambda *_: zeros, pipeline_mode=pl.Buffered(1))


def _params(*sem):
    return pltpu.CompilerParams(dimension_semantics=sem, vmem_limit_bytes=VMEM_LIMIT)


def _rwkv_proj_kernel(has_vres, *refs):
    if has_vres:
        (x_ref, halo_ref, g_ref, mix_ref, wrkv_ref, w0_ref, w1_ref, w2_ref, a0_ref, a1_ref,
         a2_ref, g1_ref, g2_ref, vf_ref, v0_ref, v1_ref, v2_ref,
         r_ref, lw_ref, k_ref, v_ref, as_ref, gate_ref) = refs
    else:
        (x_ref, halo_ref, g_ref, mix_ref, wrkv_ref, w0_ref, w1_ref, w2_ref, a0_ref, a1_ref,
         a2_ref, g1_ref, g2_ref,
         r_ref, lw_ref, k_ref, v_ref, as_ref, gate_ref) = refs
    i = pl.program_id(1)
    g = g_ref[...]
    h = _rms(x_ref[0], g)
    prev_last = _rms(halo_ref[0][7:8, :], g)
    prev_last = jnp.where(i == 0, 0.0, prev_last)
    row = lax.broadcasted_iota(jnp.int32, h.shape, 0)
    h_prev = jnp.where(row == 0, prev_last, pltpu.roll(h, 1, 0))
    xx = h_prev - h
    mix = mix_ref[...]
    xr, xk, xv, xw, xa, xg = (h + xx * mix[j:j + 1, :] for j in range(6))

    r_ref[0] = _dot(xr, wrkv_ref[0]).astype(r_ref.dtype)
    k_ref[0] = _dot(xk, wrkv_ref[1]).astype(k_ref.dtype)
    v = _dot(xv, wrkv_ref[2])
    if has_vres:
        mixv = _sigmoid(v0_ref[...] + _dot(_dot(xv, v1_ref[...]), v2_ref[...]))
        v = v + (vf_ref[0].astype(F32) - v) * mixv
    v_ref[0] = v.astype(v_ref.dtype)
    z = w0_ref[...] + _dot(jnp.tanh(_dot(xw, w1_ref[...])), w2_ref[...])
    lw_ref[0] = (-math.exp(-0.5)) * _sigmoid(z)
    a_sig = _sigmoid(a0_ref[...] + _dot(_dot(xa, a1_ref[...]), a2_ref[...]))
    as_ref[0] = a_sig.astype(as_ref.dtype)
    gate_ref[0] = _dot(_sigmoid(_dot(xg, g1_ref[...])), g2_ref[...]).astype(gate_ref.dtype)


def _rwkv_proj(x, g, mix, wrkv, w0, w1, w2, a0, a1, a2, g1, g2, vres):
    B, T, D = x.shape
    tm = TOKEN_TILE
    tile = pl.BlockSpec((1, tm, D), lambda b, i: (b, i, 0))
    halo = pl.BlockSpec((1, 8, D), lambda b, i: (b, jnp.maximum(i * (tm // 8) - 1, 0), 0))
    row = lambda a: a.reshape(1, -1)
    args = [x, x, row(g), mix, wrkv, row(w0), w1, w2, row(a0), a1, a2, g1, g2]
    specs = [tile, halo] + [_const_spec(a.shape) for a in args[2:]]
    if vres is not None:
        v_first, v0, v1, v2 = vres
        extra = [row(v0), v1, v2]
        args += [v_first] + extra
        specs += [tile] + [_const_spec(a.shape) for a in extra]
    out = [jax.ShapeDtypeStruct((B, T, D), dt) for dt in (BF16, F32, BF16, BF16, BF16, BF16)]
    return pl.pallas_call(
        functools.partial(_rwkv_proj_kernel, vres is not None),
        grid=(B, T // tm),
        in_specs=specs,
        out_specs=[tile] * 6,
        out_shape=out,
        compiler_params=_params("parallel", "arbitrary"),
        name="rwkv_proj",
    )(*args)


def _wkv_kernel(r_ref, lw_ref, k_ref, v_ref, as_ref, kk_ref, ka_ref, rk_ref, lnw_ref, lnb_ref,
                y_ref, state_ref):
    L = WKV_CHUNK
    W = WKV_GROUP_W
    groups = D_MODEL // W
    chains = [(bi, q) for bi in range(WKV_BATCH) for q in range(groups)]
    n = len(chains)
    n_double = int(math.log2(L))

    @pl.when(pl.program_id(1) == 0)
    def _():
        state_ref[...] = jnp.zeros_like(state_ref)

    ti = lax.broadcasted_iota(jnp.int32, (L, L), 0)
    tj = lax.broadcasted_iota(jnp.int32, (L, L), 1)
    tri_incl = (ti >= tj).astype(BF16)
    step_row = lax.broadcasted_iota(jnp.int32, (L, W), 0)
    step_col = lax.broadcasted_iota(jnp.int32, (L, W), 1) % L
    strict_lower = step_row > step_col
    lower = step_row >= step_col
    head_i = lax.broadcasted_iota(jnp.int32, (W, W), 0) // HEAD
    head_j = lax.broadcasted_iota(jnp.int32, (W, W), 1) // HEAD
    same_head = head_i == head_j
    head_ones = same_head.astype(BF16)

    def blockdiag(x):
        xb = x.astype(BF16)
        return jnp.where(same_head, jnp.concatenate([xb] * (W // L), axis=0), jnp.zeros((), BF16))

    def headsum(xs):
        out = jnp.dot(jnp.concatenate(xs, axis=0).astype(BF16), head_ones,
                      preferred_element_type=F32)
        return [out[i * L:(i + 1) * L] for i in range(len(xs))]

    def load(ref, c):
        return ref[c[0], :, c[1] * W:(c[1] + 1) * W].astype(F32)

    def vec(ref, c):
        return ref[:, c[1] * W:(c[1] + 1) * W]

    r = [load(r_ref, c) for c in chains]
    lw = [load(lw_ref, c) for c in chains]
    k_in = [load(k_ref, c) for c in chains]
    v = [load(v_ref, c) for c in chains]
    asig = [load(as_ref, c) for c in chains]

    kk = [k_in[i] * vec(kk_ref, c) for i, c in enumerate(chains)]
    norm2 = headsum([x * x for x in kk])
    kk = [x / jnp.maximum(jnp.sqrt(s), 1e-12) for x, s in zip(kk, norm2)]
    k = [k_in[i] * (1.0 + (asig[i] - 1.0) * vec(ka_ref, c)) for i, c in enumerate(chains)]

    cum = [_cumsum_rows(x, tri_incl) for x in lw]
    decay_to = [jnp.exp(x) for x in cum]
    decay_end = [x[L - 1:L, :] for x in decay_to]
    inv = [jnp.exp(-x) for x in cum]
    ar = [jnp.concatenate([-kk[i] * jnp.exp(cum[i] - lw[i]), r[i] * decay_to[i]],
                          axis=0).astype(BF16) for i in range(n)]
    b_s = [kk[i] * asig[i] * inv[i] for i in range(n)]
    k_s = [k[i] * inv[i] for i in range(n)]
    v_bd = [blockdiag(x) for x in v]

    s_b = [_dot_nt(ar[i], blockdiag(b_s[i])) for i in range(n)]
    s_k = [_dot_nt(ar[i], blockdiag(k_s[i])) for i in range(n)]
    s0 = [state_ref[i] for i in range(n)]
    from_state = [_dot_nt(ar[i], s0[i]) for i in range(n)]

    apow = [jnp.where(strict_lower, x[:L], 0.0) for x in s_b]
    u = [from_state[i][:L] + _dot(jnp.where(strict_lower, s_k[i][:L], 0.0), v_bd[i])
         for i in range(n)]
    for step in range(n_double):
        u = [u[i] + _dot(apow[i], blockdiag(u[i])) for i in range(n)]
        if step + 1 < n_double:
            apow = [_dot(x, blockdiag(x)) for x in apow]

    y = []
    for i in range(n):
        lhs = jnp.concatenate([jnp.where(lower, s_b[i][L:], 0.0),
                               jnp.where(lower, s_k[i][L:], 0.0)], axis=1)
        rhs = jnp.concatenate([blockdiag(u[i]), v_bd[i]], axis=0)
        y.append(from_state[i][L:] + _dot(lhs, rhs))
    for i in range(n):
        uv = jnp.concatenate([u[i], v[i]], axis=0)
        bk_end = jnp.concatenate([b_s[i], k_s[i]], axis=0) * decay_end[i]
        state_ref[i] = s0[i] * decay_end[i] + jnp.where(same_head, _dot_tn(uv, bk_end), 0.0)

    mu = headsum(y)
    yc = [y[i] - mu[i] * (1.0 / HEAD) for i in range(n)]
    var = headsum([x * x for x in yc])
    rk = headsum([r[i] * k[i] * vec(rk_ref, c) for i, c in enumerate(chains)])
    for i, c in enumerate(chains):
        yn = yc[i] * lax.rsqrt(var[i] * (1.0 / HEAD) + GN_EPS) * vec(lnw_ref, c) + vec(lnb_ref, c)
        y_ref[c[0], :, c[1] * W:(c[1] + 1) * W] = (yn + rk[i] * v[i]).astype(y_ref.dtype)


def _wkv(r, lw, k, v, asig, k_k, k_a, r_k, lnx_w, lnx_b):
    B, T, D = r.shape
    L = WKV_CHUNK
    seq = pl.BlockSpec((WKV_BATCH, L, D), lambda b, c: (b, c, 0))
    vec = pl.BlockSpec((1, D), lambda b, c: (0, 0))
    row = lambda a: a.reshape(1, D)
    n_chains = WKV_BATCH * (D // WKV_GROUP_W)
    return pl.pallas_call(
        _wkv_kernel,
        grid=(B // WKV_BATCH, T // L),
        in_specs=[seq] * 5 + [vec] * 5,
        out_specs=seq,
        out_shape=jax.ShapeDtypeStruct((B, T, D), BF16),
        scratch_shapes=[pltpu.VMEM((n_chains, WKV_GROUP_W, WKV_GROUP_W), F32)],
        compiler_params=_params("parallel", "arbitrary"),
        name="wkv7_chunked",
    )(r, lw, k, v, asig, row(k_k), row(k_a), row(r_k), row(lnx_w), row(lnx_b))


def _ffn_kernel(has_mixer_out, *refs):
    if has_mixer_out:
        (x_ref, y_ref, gate_ref, wo_ref, gmix_ref,
         gin_ref, wgu_ref, wd_ref, gout_ref, o_ref) = refs
        mixed = y_ref[...].astype(F32) * gate_ref[...].astype(F32)
        x = x_ref[...] + _rms(_dot(mixed, wo_ref[...]), gmix_ref[...])
    else:
        x_ref, gin_ref, wgu_ref, wd_ref, gout_ref, o_ref = refs
        x = x_ref[...]
    h = _rms(x, gin_ref[...]).astype(BF16)
    acc = jnp.zeros(x.shape, F32)
    for lo in range(0, D_FF, FFN_CHUNK):
        hi = min(lo + FFN_CHUNK, D_FF)
        gate = _dot(h, wgu_ref[:, lo:hi])
        up = _dot(h, wgu_ref[:, D_FF + lo:D_FF + hi])
        act = gate * _sigmoid(gate) * up
        acc = acc + _dot(act, wd_ref[lo:hi, :])
    o_ref[...] = x + _rms(acc, gout_ref[...])


def _ffn(x, mixer_out, g_in, w_gate_up, w_down, g_out):
    M, D = x.shape
    tm = FFN_TILE
    tile = pl.BlockSpec((tm, D), lambda i: (i, 0))
    acts, consts = [x], []
    if mixer_out is not None:
        y, gate, w_o, g_mix = mixer_out
        acts += [y, gate]
        consts += [w_o, g_mix.reshape(1, D)]
    consts += [g_in.reshape(1, D), w_gate_up, w_down, g_out.reshape(1, D)]
    return pl.pallas_call(
        functools.partial(_ffn_kernel, mixer_out is not None),
        grid=(M // tm,),
        in_specs=[tile] * len(acts) + [_const_spec(c.shape) for c in consts],
        out_specs=tile,
        out_shape=jax.ShapeDtypeStruct((M, D), F32),
        compiler_params=_params("parallel"),
        name="swiglu_ffn",
    )(*acts, *consts)


def _rope_tables(T):
    half = ROPE_DIM // 2
    pos = jnp.arange(T, dtype=F32)
    inv_freq = ROPE_THETA ** (-jnp.arange(0, ROPE_DIM, 2, dtype=F32) / ROPE_DIM)
    ang = pos[:, None] * inv_freq[None, :]
    cos, sin = jnp.cos(ang), jnp.sin(ang)
    rest = HEAD - ROPE_DIM
    c = jnp.concatenate([cos, cos, jnp.ones((T, rest), F32)], axis=1)
    s_next = jnp.concatenate([-sin, jnp.zeros((T, half + rest), F32)], axis=1)
    s_prev = jnp.concatenate([jnp.zeros((T, half), F32), sin, jnp.zeros((T, rest), F32)], axis=1)
    return tuple(jnp.tile(t, (1, PAIR)) for t in (c, s_next, s_prev))


def _qkv_kernel(x_ref, g_ref, w_ref, b_ref, c_ref, sn_ref, sp_ref, o_ref):
    half = ROPE_DIM // 2
    qk_dim = D_MODEL + KV_DIM
    reps = qk_dim // LANES
    h = _rms(x_ref[...], g_ref[...])
    qkv = _dot(h, w_ref[...]) + b_ref[...]
    qk = qkv[:, :qk_dim]
    c = jnp.concatenate([c_ref[...]] * reps, axis=1)
    s_next = jnp.concatenate([sn_ref[...]] * reps, axis=1)
    s_prev = jnp.concatenate([sp_ref[...]] * reps, axis=1)
    rot = (qk * c + pltpu.roll(qk, qk_dim - half, 1) * s_next + pltpu.roll(qk, half, 1) * s_prev)
    o_ref[:, :D_MODEL] = (rot[:, :D_MODEL] * ATTN_Q_SCALE).astype(o_ref.dtype)
    o_ref[:, D_MODEL:qk_dim] = rot[:, D_MODEL:].astype(o_ref.dtype)
    o_ref[:, qk_dim:] = qkv[:, qk_dim:].astype(o_ref.dtype)


def _qkv_proj(x, g, w, b, tables, T):
    M, D = x.shape
    tm = FFN_TILE
    tiles_per_seq = T // tm
    tile = pl.BlockSpec((tm, D), lambda i: (i, 0))
    tab = pl.BlockSpec((tm, LANES), lambda i: (i % tiles_per_seq, 0))
    consts = [g.reshape(1, D), w, b.reshape(1, QKV_DIM)]
    return pl.pallas_call(
        _qkv_kernel,
        grid=(M // tm,),
        in_specs=[tile] + [_const_spec(c.shape) for c in consts] + [tab] * 3,
        out_specs=pl.BlockSpec((tm, QKV_DIM), lambda i: (i, 0)),
        out_shape=jax.ShapeDtypeStruct((M, QKV_DIM), BF16),
        compiler_params=_params("parallel"),
        name="attn_qkv_rope",
    )(x, *consts, *tables)


def _attn_kernel(sinks_ref, q_ref, kp_ref, kc_ref, vp_ref, vc_ref, x_ref, w_ref, b_ref, g_ref,
                 o_ref):
    first = pl.program_id(1) * ATTN_BLOCKS
    rows = PAIR * BLOCK
    qi = lax.broadcasted_iota(jnp.int32, (rows, 2 * BLOCK), 0) % BLOCK
    kj = lax.broadcasted_iota(jnp.int32, (rows, 2 * BLOCK), 1)
    rel = qi + BLOCK - kj
    band = (rel >= 0) & (rel < WINDOW)
    top = lax.broadcasted_iota(jnp.int32, (rows, 1), 0) < BLOCK
    lane_lo = lax.broadcasted_iota(jnp.int32, (1, LANES), 1) < HEAD

    def both_halves(tile):
        swapped = pltpu.roll(tile, HEAD, 1)
        return jnp.where(lane_lo, tile, swapped), jnp.where(lane_lo, swapped, tile)

    keys = jnp.concatenate([kp_ref[0], kc_ref[0]], axis=0).astype(F32)
    vals = jnp.concatenate([vp_ref[0], vc_ref[0]], axis=0).astype(F32)
    ones = jnp.ones((keys.shape[0], LANES), BF16)
    k_heads, v_heads = [], []
    for t in range(KV_DIM // LANES):
        k_heads += [x.astype(BF16) for x in both_halves(keys[:, t * LANES:(t + 1) * LANES])]
        v_heads += [jnp.concatenate([x.astype(BF16), ones], axis=1)
                    for x in both_halves(vals[:, t * LANES:(t + 1) * LANES])]

    sink_slot =lax.broadcasted_iota(jnp.int32, (rows, LANES), 1) == 0
    slot_row = (lax.broadcasted_iota(jnp.int32, (2 * BLOCK, 2 * LANES), 0) == 0) & (
        lax.broadcasted_iota(jnp.int32, (2 * BLOCK, 2 * LANES), 1) < LANES)
    no_key = jnp.full((rows, LANES), -jnp.inf, F32)
    v_win = [[jnp.where(slot_row, jnp.zeros((), BF16), vh[blk * BLOCK:(blk + 2) * BLOCK])
              for vh in v_heads] for blk in range(ATTN_BLOCKS)]

    def head_pairs(work):
        masked = []
        for blk, t in work:
            qt = q_ref[0, blk * BLOCK:(blk + 1) * BLOCK, t * LANES:(t + 1) * LANES].astype(F32)
            q2 = jnp.concatenate([jnp.where(lane_lo, qt, 0.0), jnp.where(lane_lo, 0.0, qt)], axis=0)
            hk = (t * PAIR) // GROUP
            s = _dot_nt(q2, k_heads[hk][blk * BLOCK:(blk + 2) * BLOCK])
            sink = jnp.where(top, sinks_ref[t * PAIR], sinks_ref[t * PAIR + 1]) * LOG2E
            fill = jnp.concatenate([jnp.where(sink_slot, sink, -jnp.inf), no_key], axis=1)
            valid = band & ((first + blk > 0) | (kj >= BLOCK))
            masked.append(jnp.where(valid, s, fill))
        mx = [jnp.max(x, axis=-1, keepdims=True) for x in masked]
        p = [jnp.exp2(masked[i] - mx[i]) for i in range(len(work))]
        pv = [_dot(p[i], v_win[blk][(t * PAIR) // GROUP])
              for i, (blk, t) in enumerate(work)]
        out = []
        for i in range(len(work)):
            o2 = pv[i][:, :LANES] / pv[i][:, LANES:]
            out.append(jnp.where(lane_lo, o2[:BLOCK], o2[BLOCK:]))
        return out

    work = [(blk, t) for blk in range(ATTN_BLOCKS) for t in range(D_MODEL // LANES)]
    tiles = []
    for i in range(0, len(work), ATTN_OVERLAP):
        tiles += head_pairs(work[i:i + ATTN_OVERLAP])
    n_t = D_MODEL // LANES
    o = jnp.concatenate([jnp.concatenate(tiles[blk * n_t:(blk + 1) * n_t], axis=1)
                         for blk in range(ATTN_BLOCKS)], axis=0)
    out = _dot(o, w_ref[...]) + b_ref[...]
    o_ref[0] = x_ref[0] + _rms(out, g_ref[...])


def _attention(qkv, x, sinks, w_o, b_o, g):
    B, T, D = x.shape
    rows = ATTN_BLOCKS * BLOCK
    k_blk = D_MODEL // KV_DIM
    v_blk = k_blk + 1
    cur = lambda b, n: (b, n, 0)
    prev = lambda lane_blk: (lambda b, n: (b, jnp.maximum(n * ATTN_BLOCKS - 1, 0), lane_blk))
    consts = [w_o, b_o.reshape(1, D), g.reshape(1, D)]
    return pl.pallas_call(
        _attn_kernel,
        grid=(B, T // rows),
        in_specs=[
            pl.BlockSpec(memory_space=pltpu.SMEM),
            pl.BlockSpec((1, rows, D), cur),
            pl.BlockSpec((1, BLOCK, KV_DIM), prev(k_blk)),
            pl.BlockSpec((1, rows, KV_DIM), lambda b, n: (b, n, k_blk)),
            pl.BlockSpec((1, BLOCK, KV_DIM), prev(v_blk)),
            pl.BlockSpec((1, rows, KV_DIM), lambda b, n: (b, n, v_blk)),
            pl.BlockSpec((1, rows, D), cur),
        ] + [_const_spec(c.shape) for c in consts],
        out_specs=pl.BlockSpec((1, rows, D), cur),
        out_shape=jax.ShapeDtypeStruct((B, T, D), F32),
        compiler_params=_params("parallel", "arbitrary"),
        name="swa_sink_attention",
    )(sinks, qkv, qkv, qkv, qkv, qkv, x, *consts)


def kernel(x, norm_g, rwkv_x_mix, rwkv_w_rkv, rwkv_w0, rwkv_w1, rwkv_w2, rwkv_a0, rwkv_a1,
           rwkv_a2, rwkv_v0, rwkv_v1, rwkv_v2, rwkv_g1, rwkv_g2, rwkv_k_k, rwkv_k_a, rwkv_r_k,
           rwkv_lnx_w, rwkv_lnx_b, rwkv_w_o, attn_w_qkv, attn_b_qkv, attn_sinks, attn_w_o,
           attn_b_o, ffn_w_gate_up, ffn_w_down):
    B, T, D = x.shape
    M = B * T
    depth = norm_g.shape[0]
    tables = _rope_tables(T)
    v_first = None
    x = x.reshape(M, D)
    for i in range(depth):
        j = i // 2
        mixer_out = None
        if i % 2 == 0:
            vres = None if j == 0 else (v_first, rwkv_v0[j - 1], rwkv_v1[j - 1], rwkv_v2[j - 1])
            r, lw, k, v, asig, gate = _rwkv_proj(
                x.reshape(B, T, D), norm_g[i, 0], rwkv_x_mix[j], rwkv_w_rkv[j], rwkv_w0[j],
                rwkv_w1[j], rwkv_w2[j], rwkv_a0[j], rwkv_a1[j], rwkv_a2[j], rwkv_g1[j],
                rwkv_g2[j], vres)
            if j == 0:
                v_first = v
            y = _wkv(r, lw, k, v, asig, rwkv_k_k[j], rwkv_k_a[j], rwkv_r_k[j].reshape(D),
                     rwkv_lnx_w[j], rwkv_lnx_b[j])
            mixer_out = (y.reshape(M, D), gate.reshape(M, D), rwkv_w_o[j], norm_g[i, 1])
        else:
            qkv = _qkv_proj(x, norm_g[i, 0], attn_w_qkv[j], attn_b_qkv[j], tables, T)
            x = _attention(qkv.reshape(B, T, QKV_DIM), x.reshape(B, T, D), attn_sinks[j],
                           attn_w_o[j], attn_b_o[j], norm_g[i, 1]).reshape(M, D)
        x = _ffn(x, mixer_out, norm_g[i, 2], ffn_w_gate_up[i], ffn_w_down[i], norm_g[i, 3])
    return x.reshape(B, T, D)
```

```python
import functools
import math
from typing import NamedTuple

import jax
import jax.numpy as jnp
from jax import lax
from jax.experimental import pallas as pl
from jax.experimental.pallas import tpu as pltpu

D_MODEL = 1024
HEAD = 64
N_HEADS = D_MODEL // HEAD
N_KV_HEADS = 4
GROUP = N_HEADS // N_KV_HEADS
KV_DIM = N_KV_HEADS * HEAD
QKV_DIM = D_MODEL + 2 * KV_DIM
WINDOW = 128
BLOCK = 128
ROPE_THETA = 500000.0
ROPE_DIM = HEAD // 4
D_FF = 2816
RMS_EPS = 1e-6
GN_EPS = 64e-5
LOG2E = math.log2(math.e)
ATTN_Q_SCALE = HEAD ** -0.5 * LOG2E

LANES = 128
PAIR = LANES // HEAD
WKV_CHUNK = 64
WKV_GROUP_W = 256
WKV_BATCH = 4
ATTN_BLOCKS = 2
ATTN_OVERLAP = 8
TOKEN_TILE = 256
FFN_TILE = 512
FFN_CHUNK = 512
VMEM_LIMIT = 56 * 1024 * 1024

BF16 = jnp.bfloat16
F32 = jnp.float32

assert WKV_CHUNK == HEAD
assert WINDOW <= BLOCK


def _dot(a, b):
    return jnp.dot(a.astype(BF16), b.astype(BF16), preferred_element_type=F32)


def _dot_nt(a, b):
    return lax.dot_general(a.astype(BF16), b.astype(BF16), (((1,), (1,)), ((), ())),
                           preferred_element_type=F32)


def _dot_tn(a, b):
    return lax.dot_general(a.astype(BF16), b.astype(BF16), (((0,), (0,)), ((), ())),
                           preferred_element_type=F32)


def _cumsum_rows(x, tri01):
    hi = x.astype(BF16)
    lo = (x - hi.astype(F32)).astype(BF16)
    return jnp.dot(jnp.concatenate([tri01, tri01], axis=1), jnp.concatenate([hi, lo], axis=0),
                   preferred_element_type=F32)


def _rms(x, g):
    return x * lax.rsqrt(jnp.mean(x * x, axis=-1, keepdims=True) + RMS_EPS) * g


def _sigmoid(x):
    return 1.0 / (1.0 + jnp.exp(-x))


class _Layer(NamedTuple):
    stack: jax.Array
    index: int


def _layer(stack, index):
    if stack.ndim == 2:
        stack = stack.reshape(stack.shape[0], 1, stack.shape[1])
    return _Layer(stack, index)


def _layer_spec(p):
    shape = p.stack.shape[1:]
    where = (p.index,) + (0,) * len(shape)
    return pl.BlockSpec((None,) + shape, lambda *_: where, pipeline_mode=pl.Buffered(1))


def _params(*sem):
    return pltpu.CompilerParams(dimension_semantics=sem, vmem_limit_bytes=VMEM_LIMIT)


def _rwkv_proj_kernel(has_vres, *refs):
    if has_vres:
        (x_ref, halo_ref, g_ref, mix_ref, wrkv_ref, w0_ref, w1_ref, w2_ref, a0_ref, a1_ref,
         a2_ref, g1_ref, g2_ref, vf_ref, v0_ref, v1_ref, v2_ref,
         r_ref, lw_ref, k_ref, v_ref, as_ref, gate_ref) = refs
    else:
        (x_ref, halo_ref, g_ref, mix_ref, wrkv_ref, w0_ref, w1_ref, w2_ref, a0_ref, a1_ref,
         a2_ref, g1_ref, g2_ref,
         r_ref, lw_ref, k_ref, v_ref, as_ref, gate_ref) = refs
    i = pl.program_id(1)
    g = g_ref[...]
    h = _rms(x_ref[0], g)
    prev_last = _rms(halo_ref[0][7:8, :], g)
    prev_last = jnp.where(i == 0, 0.0, prev_last)
    row = lax.broadcasted_iota(jnp.int32, h.shape, 0)
    h_prev = jnp.where(row == 0, prev_last, pltpu.roll(h, 1, 0))
    xx = h_prev - h
    mix = mix_ref[...]
    xr, xk, xv, xw, xa, xg = (h + xx * mix[j:j + 1, :] for j in range(6))

    r_ref[0] = _dot(xr, wrkv_ref[0]).astype(r_ref.dtype)
    k_ref[0] = _dot(xk, wrkv_ref[1]).astype(k_ref.dtype)
    v = _dot(xv, wrkv_ref[2])
    if has_vres:
        mixv = _sigmoid(v0_ref[...] + _dot(_dot(xv, v1_ref[...]), v2_ref[...]))
        v = v + (vf_ref[0].astype(F32) - v) * mixv
    v_ref[0] = v.astype(v_ref.dtype)
    z = w0_ref[...] + _dot(jnp.tanh(_dot(xw, w1_ref[...])), w2_ref[...])
    lw_ref[0] = (-math.exp(-0.5)) * _sigmoid(z)
    a_sig = _sigmoid(a0_ref[...] + _dot(_dot(xa, a1_ref[...]), a2_ref[...]))
    as_ref[0] = a_sig.astype(as_ref.dtype)
    gate_ref[0] = _dot(_sigmoid(_dot(xg, g1_ref[...])), g2_ref[...]).astype(gate_ref.dtype)


def _rwkv_proj(x, g, mix, wrkv, w0, w1, w2, a0, a1, a2, g1, g2, vres):
    B, T, D = x.shape
    tm = TOKEN_TILE
    tile = pl.BlockSpec((1, tm, D), lambda b, i: (b, i, 0))
    halo = pl.BlockSpec((1, 8, D), lambda b, i: (b, jnp.maximum(i * (tm // 8) - 1, 0), 0))
    layer = [g, mix, wrkv, w0, w1, w2, a0, a1, a2, g1, g2]
    args = [x, x] + [p.stack for p in layer]
    specs = [tile, halo] + [_layer_spec(p) for p in layer]
    if vres is not None:
        v_first, v0, v1, v2 = vres
        args += [v_first] + [p.stack for p in (v0, v1, v2)]
        specs += [tile] + [_layer_spec(p) for p in (v0, v1, v2)]
    out = [jax.ShapeDtypeStruct((B, T, D), dt) for dt in (BF16, F32, BF16, BF16, BF16, BF16)]
    return pl.pallas_call(
        functools.partial(_rwkv_proj_kernel, vres is not None),
        grid=(B, T // tm),
        in_specs=specs,
        out_specs=[tile] * 6,
        out_shape=out,
        compiler_params=_params("parallel", "arbitrary"),
        name="rwkv_proj",
    )(*args)


def _wkv_kernel(r_ref, lw_ref, k_ref, v_ref, as_ref, kk_ref, ka_ref, rk_ref, lnw_ref, lnb_ref,
                y_ref, state_ref):
    L = WKV_CHUNK
    W = WKV_GROUP_W
    groups = D_MODEL // W
    chains = [(bi, q) for bi in range(WKV_BATCH) for q in range(groups)]
    n = len(chains)
    n_double = int(math.log2(L))

    @pl.when(pl.program_id(1) == 0)
    def _():
        state_ref[...] = jnp.zeros_like(state_ref)

    ti = lax.broadcasted_iota(jnp.int32, (L, L), 0)
    tj = lax.broadcasted_iota(jnp.int32, (L, L), 1)
    tri_incl = (ti >= tj).astype(BF16)
    step_row = lax.broadcasted_iota(jnp.int32, (L, W), 0)
    step_col = lax.broadcasted_iota(jnp.int32, (L, W), 1) % L
    strict_lower = step_row > step_col
    lower = step_row >= step_col
    diagonal = step_row == step_col
    head_i = lax.broadcasted_iota(jnp.int32, (W, W), 0) // HEAD
    head_j = lax.broadcasted_iota(jnp.int32, (W, W), 1) // HEAD
    same_head = head_i == head_j
    head_ones = same_head.astype(BF16)

    def blockdiag(x):
        xb = x.astype(BF16)
        return jnp.where(same_head, jnp.concatenate([xb] * (W // L), axis=0), jnp.zeros((), BF16))

    def headsum(xs):
        out = jnp.dot(jnp.concatenate(xs, axis=0).astype(BF16), head_ones,
                      preferred_element_type=F32)
        return [out[i * L:(i + 1) * L] for i in range(len(xs))]

    def load(ref, c):
        return ref[c[0], :, c[1] * W:(c[1] + 1) * W].astype(F32)

    def vec(ref, c):
        return ref[:, c[1] * W:(c[1] + 1) * W]

    r = [load(r_ref, c) for c in chains]
    lw = [load(lw_ref, c) for c in chains]
    k_in = [load(k_ref, c) for c in chains]
    v = [load(v_ref, c) for c in chains]
    asig = [load(as_ref, c) for c in chains]

    kk = [k_in[i] * vec(kk_ref, c) for i, c in enumerate(chains)]
    norm2 = headsum([x * x for x in kk])
    kk = [x / jnp.maximum(jnp.sqrt(s), 1e-12) for x, s in zip(kk, norm2)]
    k = [k_in[i] * (1.0 + (asig[i] - 1.0) * vec(ka_ref, c)) for i, c in enumerate(chains)]

    cum = [_cumsum_rows(x, tri_incl) for x in lw]
    decay_to = [jnp.exp(x) for x in cum]
    decay_end = [x[L - 1:L, :] for x in decay_to]
    inv = [jnp.exp(-x) for x in cum]
    ar = [jnp.concatenate([-kk[i] * jnp.exp(cum[i] - lw[i]), r[i] * decay_to[i]],
                          axis=0).astype(BF16) for i in range(n)]
    b_s = [kk[i] * asig[i] * inv[i] for i in range(n)]
    k_s = [k[i] * inv[i] for i in range(n)]
    v_bd = [blockdiag(x) for x in v]

    s_b = [_dot_nt(ar[i], blockdiag(b_s[i])) for i in range(n)]
    s_k = [_dot_nt(ar[i], blockdiag(k_s[i])) for i in range(n)]
    s0 = [state_ref[i] for i in range(n)]
    from_state = [_dot_nt(ar[i], s0[i]) for i in range(n)]

    apow = [jnp.where(strict_lower, x[:L], 0.0) for x in s_b]
    tinv = [jnp.where(diagonal, 1.0, a) for a in apow]
    apow = [_dot(a, blockdiag(a)) for a in apow]
    for step in range(1, n_double):
        power_bd = [blockdiag(a) for a in apow]
        if step + 1 < n_double:
            both = [_dot(jnp.concatenate([apow[i], tinv[i]], axis=0), power_bd[i])
                    for i in range(n)]
            apow = [x[:L] for x in both]
            tinv = [tinv[i] + both[i][L:] for i in range(n)]
        else:
            tinv = [tinv[i] + _dot(tinv[i], power_bd[i]) for i in range(n)]

    with_v = [_dot(jnp.concatenate([jnp.where(strict_lower, s_k[i][:L], 0.0),
                                    jnp.where(lower, s_k[i][L:], 0.0)], axis=0), v_bd[i])
              for i in range(n)]
    u = [_dot(tinv[i], blockdiag(from_state[i][:L] + with_v[i][:L])) for i in range(n)]
    y = [from_state[i][L:] + with_v[i][L:]
         + _dot(jnp.where(lower, s_b[i][L:], 0.0), blockdiag(u[i])) for i in range(n)]
    for i in range(n):
        uv = jnp.concatenate([u[i], v[i]], axis=0)
        bk_end = jnp.concatenate([b_s[i], k_s[i]], axis=0) * decay_end[i]
        state_ref[i] = s0[i] * decay_end[i] + jnp.where(same_head, _dot_tn(uv, bk_end), 0.0)

    mu = headsum(y)
    yc = [y[i] - mu[i] * (1.0 / HEAD) for i in range(n)]
    var = headsum([x * x for x in yc])
    rk = headsum([r[i] * k[i] * vec(rk_ref, c) for i, c in enumerate(chains)])
    for i, c in enumerate(chains):
        yn = yc[i] * lax.rsqrt(var[i] * (1.0 / HEAD) + GN_EPS) * vec(lnw_ref, c) + vec(lnb_ref, c)
        y_ref[c[0], :, c[1] * W:(c[1] + 1) * W] = (yn + rk[i] * v[i]).astype(y_ref.dtype)


def _wkv(r, lw, k, v, asig, k_k, k_a, r_k, lnx_w, lnx_b):
    B, T, D = r.shape
    L = WKV_CHUNK
    seq = pl.BlockSpec((WKV_BATCH, L, D), lambda b, c: (b, c, 0))
    vecs = [k_k, k_a, r_k, lnx_w, lnx_b]
    n_chains = WKV_BATCH * (D // WKV_GROUP_W)
    return pl.pallas_call(
        _wkv_kernel,
        grid=(B // WKV_BATCH, T // L),
        in_specs=[seq] * 5 + [_layer_spec(p) for p in vecs],
        out_specs=seq,
        out_shape=jax.ShapeDtypeStruct((B, T, D), BF16),
        scratch_shapes=[pltpu.VMEM((n_chains, WKV_GROUP_W, WKV_GROUP_W), F32)],
        compiler_params=_params("parallel", "arbitrary"),
        name="wkv7_chunked",
    )(r, lw, k, v, asig, *[p.stack for p in vecs])


def _ffn_kernel(has_mixer_out, *refs):
    if has_mixer_out:
        (x_ref, y_ref, gate_ref, wo_ref, gmix_ref,
         gin_ref, wgu_ref, wd_ref, gout_ref, o_ref) = refs
        mixed = y_ref[...].astype(F32) * gate_ref[...].astype(F32)
        x = x_ref[...] + _rms(_dot(mixed, wo_ref[...]), gmix_ref[...])
    else:
        x_ref, gin_ref, wgu_ref, wd_ref, gout_ref, o_ref = refs
        x = x_ref[...]
    h = _rms(x, gin_ref[...]).astype(BF16)
    acc = jnp.zeros(x.shape, F32)
    for lo in range(0, D_FF, FFN_CHUNK):
        hi = min(lo + FFN_CHUNK, D_FF)
        gate = _dot(h, wgu_ref[:, lo:hi])
        up = _dot(h, wgu_ref[:, D_FF + lo:D_FF + hi])
        act = gate * _sigmoid(gate) * up
        acc = acc + _dot(act, wd_ref[lo:hi, :])
    o_ref[...] = x + _rms(acc, gout_ref[...])


def _ffn(x, mixer_out, g_in, w_gate_up, w_down, g_out):
    M, D = x.shape
    tm = FFN_TILE
    tile = pl.BlockSpec((tm, D), lambda i: (i, 0))
    acts, consts = [x], []
    if mixer_out is not None:
        y, gate, w_o, g_mix = mixer_out
        acts += [y, gate]
        consts += [w_o, g_mix]
    consts += [g_in, w_gate_up, w_down, g_out]
    return pl.pallas_call(
        functools.partial(_ffn_kernel, mixer_out is not None),
        grid=(M // tm,),
        in_specs=[tile] * len(acts) + [_layer_spec(p) for p in consts],
        out_specs=tile,
        out_shape=jax.ShapeDtypeStruct((M, D), F32),
        compiler_params=_params("parallel"),
        name="swiglu_ffn",
    )(*acts, *[p.stack for p in consts])


def _rope_tables(T):
    half = ROPE_DIM // 2
    pos = jnp.arange(T, dtype=F32)
    inv_freq = ROPE_THETA ** (-jnp.arange(0, ROPE_DIM, 2, dtype=F32) / ROPE_DIM)
    ang = pos[:, None] * inv_freq[None, :]
    cos, sin = jnp.cos(ang), jnp.sin(ang)
    rest = HEAD - ROPE_DIM
    c = jnp.concatenate([cos, cos, jnp.ones((T, rest), F32)], axis=1)
    s_next = jnp.concatenate([-sin, jnp.zeros((T, half + rest), F32)], axis=1)
    s_prev = jnp.concatenate([jnp.zeros((T, half), F32), sin, jnp.zeros((T, rest), F32)], axis=1)
    return tuple(jnp.tile(t, (1, PAIR)) for t in (c, s_next, s_prev))


def _qkv_kernel(x_ref, g_ref, w_ref, b_ref, c_ref, sn_ref, sp_ref, o_ref):
    half = ROPE_DIM // 2
    qk_dim = D_MODEL + KV_DIM
    reps = qk_dim // LANES
    h = _rms(x_ref[...], g_ref[...])
    qkv = _dot(h, w_ref[...]) + b_ref[...]
    qk = qkv[:, :qk_dim]
    c = jnp.concatenate([c_ref[...]] * reps, axis=1)
    s_next = jnp.concatenate([sn_ref[...]] * reps, axis=1)
    s_prev = jnp.concatenate([sp_ref[...]] * reps, axis=1)
    rot = (qk * c + pltpu.roll(qk, qk_dim - half, 1) * s_next + pltpu.roll(qk, half, 1) * s_prev)
    o_ref[:, :D_MODEL] = (rot[:, :D_MODEL] * ATTN_Q_SCALE).astype(o_ref.dtype)
    o_ref[:, D_MODEL:qk_dim] = rot[:, D_MODEL:].astype(o_ref.dtype)
    o_ref[:, qk_dim:] = qkv[:, qk_dim:].astype(o_ref.dtype)


def _qkv_proj(x, g, w, b, tables, T):
    M, D = x.shape
    tm = FFN_TILE
    tiles_per_seq = T // tm
    tile = pl.BlockSpec((tm, D), lambda i: (i, 0))
    tab = pl.BlockSpec((tm, LANES), lambda i: (i % tiles_per_seq, 0))
    consts = [g, w, b]
    return pl.pallas_call(
        _qkv_kernel,
        grid=(M // tm,),
        in_specs=[tile] + [_layer_spec(p) for p in consts] + [tab] * 3,
        out_specs=pl.BlockSpec((tm, QKV_DIM), lambda i: (i, 0)),
        out_shape=jax.ShapeDtypeStruct((M, QKV_DIM), BF16),
        compiler_params=_params("parallel"),
        name="attn_qkv_rope",
    )(x, *[p.stack for p in consts], *tables)


def _attn_kernel(layer, sinks_ref, q_ref, kp_ref, kc_ref, vp_ref, vc_ref, x_ref, w_ref, b_ref,
                 g_ref, o_ref):
    first = pl.program_id(1) * ATTN_BLOCKS
    rows = PAIR * BLOCK
    qi = lax.broadcasted_iota(jnp.int32, (rows, 2 * BLOCK), 0) % BLOCK
    kj = lax.broadcasted_iota(jnp.int32, (rows, 2 * BLOCK), 1)
    rel = qi + BLOCK - kj
    band = (rel >= 0) & (rel < WINDOW)
    top = lax.broadcasted_iota(jnp.int32, (rows, 1), 0) < BLOCK
    lane_lo = lax.broadcasted_iota(jnp.int32, (1, LANES), 1) < HEAD

    def both_halves(tile):
        swapped = pltpu.roll(tile, HEAD, 1)
        return jnp.where(lane_lo, tile, swapped), jnp.where(lane_lo, swapped, tile)

    keys = jnp.concatenate([kp_ref[0], kc_ref[0]], axis=0).astype(F32)
    vals = jnp.concatenate([vp_ref[0], vc_ref[0]], axis=0).astype(F32)
    ones = jnp.ones((keys.shape[0], LANES), BF16)
    k_heads, v_heads = [], []
    for t in range(KV_DIM // LANES):
        k_heads += [x.astype(BF16) for x in both_halves(keys[:, t * LANES:(t + 1) * LANES])]
        v_heads += [jnp.concatenate([x.astype(BF16), ones], axis=1)
                    for x in both_halves(vals[:, t * LANES:(t + 1) * LANES])]

    sink_slot =lax.broadcasted_iota(jnp.int32, (rows, LANES), 1) == 0
    slot_row = (lax.broadcasted_iota(jnp.int32, (2 * BLOCK, 2 * LANES), 0) == 0) & (
        lax.broadcasted_iota(jnp.int32, (2 * BLOCK, 2 * LANES), 1) < LANES)
    no_key = jnp.full((rows, LANES), -jnp.inf, F32)
    v_win = [[jnp.where(slot_row, jnp.zeros((), BF16), vh[blk * BLOCK:(blk + 2) * BLOCK])
              for vh in v_heads] for blk in range(ATTN_BLOCKS)]

    def head_pairs(work):
        masked = []
        for blk, t in work:
            qt = q_ref[0, blk * BLOCK:(blk + 1) * BLOCK, t * LANES:(t + 1) * LANES].astype(F32)
            q2 = jnp.concatenate([jnp.where(lane_lo, qt, 0.0), jnp.where(lane_lo, 0.0, qt)], axis=0)
            hk = (t * PAIR) // GROUP
            s = _dot_nt(q2, k_heads[hk][blk * BLOCK:(blk + 2) * BLOCK])
            sink = jnp.where(top, sinks_ref[layer, t * PAIR],
                             sinks_ref[layer, t * PAIR + 1]) * LOG2E
            fill = jnp.concatenate([jnp.where(sink_slot, sink, -jnp.inf), no_key], axis=1)
            valid = band & ((first + blk > 0) | (kj >= BLOCK))
            masked.append(jnp.where(valid, s, fill))
        mx = [jnp.max(x, axis=-1, keepdims=True) for x in masked]
        p = [jnp.exp2(masked[i] - mx[i]) for i in range(len(work))]
        pv = [_dot(p[i], v_win[blk][(t * PAIR) // GROUP])
              for i, (blk, t) in enumerate(work)]
        out = []
        for i in range(len(work)):
            o2 = pv[i][:, :LANES] / pv[i][:, LANES:]
            out.append(jnp.where(lane_lo, o2[:BLOCK], o2[BLOCK:]))
        return out

    work = [(blk, t) for blk in range(ATTN_BLOCKS) for t in range(D_MODEL // LANES)]
    tiles = []
    for i in range(0, len(work), ATTN_OVERLAP):
        tiles += head_pairs(work[i:i + ATTN_OVERLAP])
    n_t = D_MODEL // LANES
    o = jnp.concatenate([jnp.concatenate(tiles[blk * n_t:(blk + 1) * n_t], axis=1)
                         for blk in range(ATTN_BLOCKS)], axis=0)
    out = _dot(o, w_ref[...]) + b_ref[...]
    o_ref[0] = x_ref[0] + _rms(out, g_ref[...])


def _attention(qkv, x, sinks, w_o, b_o, g):
    B, T, D = x.shape
    rows = ATTN_BLOCKS * BLOCK
    k_blk = D_MODEL // KV_DIM
    v_blk = k_blk + 1
    cur = lambda b, n: (b, n, 0)
    prev = lambda lane_blk: (lambda b, n: (b, jnp.maximum(n * ATTN_BLOCKS - 1, 0), lane_blk))
    consts = [w_o, b_o, g]
    return pl.pallas_call(
        functools.partial(_attn_kernel, sinks.index),
        grid=(B, T // rows),
        in_specs=[
            pl.BlockSpec(memory_space=pltpu.SMEM),
            pl.BlockSpec((1, rows, D), cur),
            pl.BlockSpec((1, BLOCK, KV_DIM), prev(k_blk)),
            pl.BlockSpec((1, rows, KV_DIM), lambda b, n: (b, n, k_blk)),
            pl.BlockSpec((1, BLOCK, KV_DIM), prev(v_blk)),
            pl.BlockSpec((1, rows, KV_DIM), lambda b, n: (b, n, v_blk)),
            pl.BlockSpec((1, rows, D), cur),
        ] + [_layer_spec(p) for p in consts],
        out_specs=pl.BlockSpec((1, rows, D), cur),
        out_shape=jax.ShapeDtypeStruct((B, T, D), F32),
        compiler_params=_params("parallel", "arbitrary"),
        name="swa_sink_attention",
    )(sinks.stack, qkv, qkv, qkv, qkv, qkv, x, *[p.stack for p in consts])


def kernel(x, norm_g, rwkv_x_mix, rwkv_w_rkv, rwkv_w0, rwkv_w1, rwkv_w2, rwkv_a0, rwkv_a1,
           rwkv_a2, rwkv_v0, rwkv_v1, rwkv_v2, rwkv_g1, rwkv_g2, rwkv_k_k, rwkv_k_a, rwkv_r_k,
           rwkv_lnx_w, rwkv_lnx_b, rwkv_w_o, attn_w_qkv, attn_b_qkv, attn_sinks, attn_w_o,
           attn_b_o, ffn_w_gate_up, ffn_w_down):
    B, T, D = x.shape
    M = B * T
    depth = norm_g.shape[0]
    tables = _rope_tables(T)
    v_first = None
    x = x.reshape(M, D)
    gains = norm_g.reshape(depth * 4, D)
    r_k = rwkv_r_k.reshape(rwkv_r_k.shape[0], D)
    for i in range(depth):
        j = i // 2
        gain = lambda which: _layer(gains, 4 * i + which)
        mixer_out = None
        if i % 2 == 0:
            vres = None if j == 0 else (v_first, _layer(rwkv_v0, j - 1), _layer(rwkv_v1, j - 1),
                                        _layer(rwkv_v2, j - 1))
            r, lw, k, v, asig, gate = _rwkv_proj(
                x.reshape(B, T, D), gain(0), *[_layer(p, j) for p in (
                    rwkv_x_mix, rwkv_w_rkv, rwkv_w0, rwkv_w1, rwkv_w2, rwkv_a0, rwkv_a1, rwkv_a2,
                    rwkv_g1, rwkv_g2)], vres)
            if j == 0:
                v_first = v
            y = _wkv(r, lw, k, v, asig, *[_layer(p, j) for p in (
                rwkv_k_k, rwkv_k_a, r_k, rwkv_lnx_w, rwkv_lnx_b)])
            mixer_out = (y.reshape(M, D), gate.reshape(M, D), _layer(rwkv_w_o, j), gain(1))
        else:
            qkv = _qkv_proj(x, gain(0), _layer(attn_w_qkv, j), _layer(attn_b_qkv, j), tables, T)
            x = _attention(qkv.reshape(B, T, QKV_DIM), x.reshape(B, T, D), _Layer(attn_sinks, j),
                           _layer(attn_w_o, j), _layer(attn_b_o, j), gain(1)).reshape(M, D)
        x = _ffn(x, mixer_out, gain(2), _layer(ffn_w_gate_up, i), _layer(ffn_w_down, i), gain(3))
    return x.reshape(B, T, D)
```

```python
import functools
import math
from typing import NamedTuple

import jax
import jax.numpy as jnp
from jax import lax
from jax.experimental import pallas as pl
from jax.experimental.pallas import tpu as pltpu

D_MODEL = 1024
HEAD = 64
N_HEADS = D_MODEL // HEAD
N_KV_HEADS = 4
GROUP = N_HEADS // N_KV_HEADS
KV_DIM = N_KV_HEADS * HEAD
QKV_DIM = D_MODEL + 2 * KV_DIM
WINDOW = 128
BLOCK = 128
ROPE_THETA = 500000.0
ROPE_DIM = HEAD // 4
D_FF = 2816
RMS_EPS = 1e-6
GN_EPS = 64e-5
LOG2E = math.log2(math.e)
ATTN_Q_SCALE = HEAD ** -0.5 * LOG2E

LANES = 128
PAIR = LANES // HEAD
WKV_CHUNK = 64
WKV_GROUP_W = 256
WKV_BATCH = 4
ATTN_BLOCKS = 4
ATTN_OVERLAP = 8
TOKEN_TILE = 512
QKV_TILE = 1024
FFN_TILE = 512
FFN_CHUNK = 512
VMEM_LIMIT = 56 * 1024 * 1024

BF16 = jnp.bfloat16
F32 = jnp.float32

assert WKV_CHUNK == HEAD
assert WINDOW <= BLOCK


def _dot(a, b):
    return jnp.dot(a.astype(BF16), b.astype(BF16), preferred_element_type=F32)


def _dot_nt(a, b):
    return lax.dot_general(a.astype(BF16), b.astype(BF16), (((1,), (1,)), ((), ())),
                           preferred_element_type=F32)


def _dot_tn(a, b):
    return lax.dot_general(a.astype(BF16), b.astype(BF16), (((0,), (0,)), ((), ())),
                           preferred_element_type=F32)


def _cumsum_rows(x, tri01):
    hi = x.astype(BF16)
    lo = (x - hi.astype(F32)).astype(BF16)
    return jnp.dot(jnp.concatenate([tri01, tri01], axis=1), jnp.concatenate([hi, lo], axis=0),
                   preferred_element_type=F32)


def _rms(x, g):
    return x * lax.rsqrt(jnp.mean(x * x, axis=-1, keepdims=True) + RMS_EPS) * g


def _sigmoid(x):
    return 1.0 / (1.0 + jnp.exp(-x))


class _Layer(NamedTuple):
    stack: jax.Array
    index: int


def _layer(stack, index):
    if stack.ndim == 2:
        stack = stack.reshape(stack.shape[0], 1, stack.shape[1])
    return _Layer(stack, index)


def _layer_spec(p):
    shape = p.stack.shape[1:]
    where = (p.index,) + (0,) * len(shape)
    return pl.BlockSpec((None,) + shape, lambda *_: where, pipeline_mode=pl.Buffered(1))


def _params(*sem):
    return pltpu.CompilerParams(dimension_semantics=sem, vmem_limit_bytes=VMEM_LIMIT)


def _rwkv_proj_kernel(has_vres, *refs):
    if has_vres:
        (x_ref, halo_ref, g_ref, mix_ref, wrkv_ref, w0_ref, w1_ref, w2_ref, a0_ref, a1_ref,
         a2_ref, g1_ref, g2_ref, vf_ref, v0_ref, v1_ref, v2_ref,
         r_ref, lw_ref, k_ref, v_ref, as_ref, gate_ref) = refs
    else:
        (x_ref, halo_ref, g_ref, mix_ref, wrkv_ref, w0_ref, w1_ref, w2_ref, a0_ref, a1_ref,
         a2_ref, g1_ref, g2_ref,
         r_ref, lw_ref, k_ref, v_ref, as_ref, gate_ref) = refs
    i = pl.program_id(1)
    g = g_ref[...]
    h = _rms(x_ref[0], g)
    prev_last = _rms(halo_ref[0][7:8, :], g)
    prev_last = jnp.where(i == 0, 0.0, prev_last)
    row = lax.broadcasted_iota(jnp.int32, h.shape, 0)
    h_prev = jnp.where(row == 0, prev_last, pltpu.roll(h, 1, 0))
    xx = h_prev - h
    mix = mix_ref[...]
    xr, xk, xv, xw, xa, xg = (h + xx * mix[j:j + 1, :] for j in range(6))

    r_ref[0] = _dot(xr, wrkv_ref[0]).astype(r_ref.dtype)
    k_ref[0] = _dot(xk, wrkv_ref[1]).astype(k_ref.dtype)
    v = _dot(xv, wrkv_ref[2])
    if has_vres:
        mixv = _sigmoid(v0_ref[...] + _dot(_dot(xv, v1_ref[...]), v2_ref[...]))
        v = v + (vf_ref[0].astype(F32) - v) * mixv
    v_ref[0] = v.astype(v_ref.dtype)
    z = w0_ref[...] + _dot(jnp.tanh(_dot(xw, w1_ref[...])), w2_ref[...])
    lw_ref[0] = (-math.exp(-0.5)) * _sigmoid(z)
    a_sig = _sigmoid(a0_ref[...] + _dot(_dot(xa, a1_ref[...]), a2_ref[...]))
    as_ref[0] = a_sig.astype(as_ref.dtype)
    gate_ref[0] = _dot(_sigmoid(_dot(xg, g1_ref[...])), g2_ref[...]).astype(gate_ref.dtype)


def _rwkv_proj(x, g, mix, wrkv, w0, w1, w2, a0, a1, a2, g1, g2, vres):
    B, T, D = x.shape
    tm = TOKEN_TILE
    tile = pl.BlockSpec((1, tm, D), lambda b, i: (b, i, 0))
    halo = pl.BlockSpec((1, 8, D), lambda b, i: (b, jnp.maximum(i * (tm // 8) - 1, 0), 0))
    layer = [g, mix, wrkv, w0, w1, w2, a0, a1, a2, g1, g2]
    args = [x, x] + [p.stack for p in layer]
    specs = [tile, halo] + [_layer_spec(p) for p in layer]
    if vres is not None:
        v_first, v0, v1, v2 = vres
        args += [v_first] + [p.stack for p in (v0, v1, v2)]
        specs += [tile] + [_layer_spec(p) for p in (v0, v1, v2)]
    out = [jax.ShapeDtypeStruct((B, T, D), dt) for dt in (BF16, F32, BF16, BF16, BF16, BF16)]
    return pl.pallas_call(
        functools.partial(_rwkv_proj_kernel, vres is not None),
        grid=(B, T // tm),
        in_specs=specs,
        out_specs=[tile] * 6,
        out_shape=out,
        compiler_params=_params("parallel", "arbitrary"),
        name="rwkv_proj",
    )(*args)


def _wkv_kernel(r_ref, lw_ref, k_ref, v_ref, as_ref, kk_ref, ka_ref, rk_ref, lnw_ref, lnb_ref,
                y_ref, state_ref):
    L = WKV_CHUNK
    W = WKV_GROUP_W
    groups = D_MODEL // W
    chains = [(bi, q) for bi in range(WKV_BATCH) for q in range(groups)]
    n = len(chains)
    n_double = int(math.log2(L))

    @pl.when(pl.program_id(1) == 0)
    def _():
        state_ref[...] = jnp.zeros_like(state_ref)

    ti = lax.broadcasted_iota(jnp.int32, (L, L), 0)
    tj = lax.broadcasted_iota(jnp.int32, (L, L), 1)
    tri_incl = (ti >= tj).astype(BF16)
    step_row = lax.broadcasted_iota(jnp.int32, (L, W), 0)
    step_col = lax.broadcasted_iota(jnp.int32, (L, W), 1) % L
    strict_lower = step_row > step_col
    lower = step_row >= step_col
    diagonal = step_row == step_col
    head_i = lax.broadcasted_iota(jnp.int32, (W, W), 0) // HEAD
    head_j = lax.broadcasted_iota(jnp.int32, (W, W), 1) // HEAD
    same_head = head_i == head_j
    head_ones = same_head.astype(BF16)

    def blockdiag(x):
        xb = x.astype(BF16)
        return jnp.where(same_head, jnp.concatenate([xb] * (W // L), axis=0), jnp.zeros((), BF16))

    def headsum(xs):
        out = jnp.dot(jnp.concatenate(xs, axis=0).astype(BF16), head_ones,
                      preferred_element_type=F32)
        return [out[i * L:(i + 1) * L] for i in range(len(xs))]

    def load(ref, c):
        return ref[c[0], :, c[1] * W:(c[1] + 1) * W].astype(F32)

    def vec(ref, c):
        return ref[:, c[1] * W:(c[1] + 1) * W]

    r = [load(r_ref, c) for c in chains]
    lw = [load(lw_ref, c) for c in chains]
    k_in = [load(k_ref, c) for c in chains]
    v = [load(v_ref, c) for c in chains]
    asig = [load(as_ref, c) for c in chains]

    kk = [k_in[i] * vec(kk_ref, c) for i, c in enumerate(chains)]
    norm2 = headsum([x * x for x in kk])
    kk = [x / jnp.maximum(jnp.sqrt(s), 1e-12) for x, s in zip(kk, norm2)]
    k = [k_in[i] * (1.0 + (asig[i] - 1.0) * vec(ka_ref, c)) for i, c in enumerate(chains)]

    cum = [_cumsum_rows(x, tri_incl) for x in lw]
    decay_to = [jnp.exp(x) for x in cum]
    decay_end = [x[L - 1:L, :] for x in decay_to]
    inv = [jnp.exp(-x) for x in cum]
    ar = [jnp.concatenate([-kk[i] * jnp.exp(cum[i] - lw[i]), r[i] * decay_to[i]],
                          axis=0).astype(BF16) for i in range(n)]
    b_s = [kk[i] * asig[i] * inv[i] for i in range(n)]
    k_s = [k[i] * inv[i] for i in range(n)]
    v_bd = [blockdiag(x) for x in v]

    s_b = [_dot_nt(ar[i], blockdiag(b_s[i])) for i in range(n)]
    s_k = [_dot_nt(ar[i], blockdiag(k_s[i])) for i in range(n)]
    s0 = [state_ref[i] for i in range(n)]
    from_state = [_dot_nt(ar[i], s0[i]) for i in range(n)]

    apow = [jnp.where(strict_lower, x[:L], 0.0) for x in s_b]
    tinv = [jnp.where(diagonal, 1.0, a) for a in apow]
    apow = [_dot(a, blockdiag(a)) for a in apow]
    for step in range(1, n_double):
        power_bd = [blockdiag(a) for a in apow]
        if step + 1 < n_double:
            both = [_dot(jnp.concatenate([apow[i], tinv[i]], axis=0), power_bd[i])
                    for i in range(n)]
            apow = [x[:L] for x in both]
            tinv = [tinv[i] + both[i][L:] for i in range(n)]
        else:
            tinv = [tinv[i] + _dot(tinv[i], power_bd[i]) for i in range(n)]

    with_v = [_dot(jnp.concatenate([jnp.where(strict_lower, s_k[i][:L], 0.0),
                                    jnp.where(lower, s_k[i][L:], 0.0)], axis=0), v_bd[i])
              for i in range(n)]
    u = [_dot(tinv[i], blockdiag(from_state[i][:L] + with_v[i][:L])) for i in range(n)]
    y = [from_state[i][L:] + with_v[i][L:]
         + _dot(jnp.where(lower, s_b[i][L:], 0.0), blockdiag(u[i])) for i in range(n)]
    for i in range(n):
        uv = jnp.concatenate([u[i], v[i]], axis=0)
        bk_end = jnp.concatenate([b_s[i], k_s[i]], axis=0) * decay_end[i]
        state_ref[i] = s0[i] * decay_end[i] + jnp.where(same_head, _dot_tn(uv, bk_end), 0.0)

    mu = headsum(y)
    yc = [y[i] - mu[i] * (1.0 / HEAD) for i in range(n)]
    var = headsum([x * x for x in yc])
    rk = headsum([r[i] * k[i] * vec(rk_ref, c) for i, c in enumerate(chains)])
    for i, c in enumerate(chains):
        yn = yc[i] * lax.rsqrt(var[i] * (1.0 / HEAD) + GN_EPS) * vec(lnw_ref, c) + vec(lnb_ref, c)
        y_ref[c[0], :, c[1] * W:(c[1] + 1) * W] = (yn + rk[i] * v[i]).astype(y_ref.dtype)


def _wkv(r, lw, k, v, asig, k_k, k_a, r_k, lnx_w, lnx_b):
    B, T, D = r.shape
    L = WKV_CHUNK
    seq = pl.BlockSpec((WKV_BATCH, L, D), lambda b, c: (b, c, 0))
    vecs = [k_k, k_a, r_k, lnx_w, lnx_b]
    n_chains = WKV_BATCH * (D // WKV_GROUP_W)
    return pl.pallas_call(
        _wkv_kernel,
        grid=(B // WKV_BATCH, T // L),
        in_specs=[seq] * 5 + [_layer_spec(p) for p in vecs],
        out_specs=seq,
        out_shape=jax.ShapeDtypeStruct((B, T, D), BF16),
        scratch_shapes=[pltpu.VMEM((n_chains, WKV_GROUP_W, WKV_GROUP_W), F32)],
        compiler_params=_params("parallel", "arbitrary"),
        name="wkv7_chunked",
    )(r, lw, k, v, asig, *[p.stack for p in vecs])


def _ffn_kernel(has_mixer_out, *refs):
    if has_mixer_out:
        (x_ref, y_ref, gate_ref, wo_ref, gmix_ref,
         gin_ref, wgu_ref, wd_ref, gout_ref, o_ref) = refs
        mixed = y_ref[...].astype(F32) * gate_ref[...].astype(F32)
        x = x_ref[...] + _rms(_dot(mixed, wo_ref[...]), gmix_ref[...])
    else:
        x_ref, gin_ref, wgu_ref, wd_ref, gout_ref, o_ref = refs
        x = x_ref[...]
    h = _rms(x, gin_ref[...]).astype(BF16)
    acc = jnp.zeros(x.shape, F32)
    for lo in range(0, D_FF, FFN_CHUNK):
        hi = min(lo + FFN_CHUNK, D_FF)
        gate = _dot(h, wgu_ref[:, lo:hi])
        up = _dot(h, wgu_ref[:, D_FF + lo:D_FF + hi])
        act = gate * _sigmoid(gate) * up
        acc = acc + _dot(act, wd_ref[lo:hi, :])
    o_ref[...] = x + _rms(acc, gout_ref[...])


def _ffn(x, mixer_out, g_in, w_gate_up, w_down, g_out):
    M, D = x.shape
    tm = FFN_TILE
    tile = pl.BlockSpec((tm, D), lambda i: (i, 0))
    acts, consts = [x], []
    if mixer_out is not None:
        y, gate, w_o, g_mix = mixer_out
        acts += [y, gate]
        consts += [w_o, g_mix]
    consts += [g_in, w_gate_up, w_down, g_out]
    return pl.pallas_call(
        functools.partial(_ffn_kernel, mixer_out is not None),
        grid=(M // tm,),
        in_specs=[tile] * len(acts) + [_layer_spec(p) for p in consts],
        out_specs=tile,
        out_shape=jax.ShapeDtypeStruct((M, D), F32),
        compiler_params=_params("parallel"),
        name="swiglu_ffn",
    )(*acts, *[p.stack for p in consts])


def _rope_tables(T):
    half = ROPE_DIM // 2
    pos = jnp.arange(T, dtype=F32)
    inv_freq = ROPE_THETA ** (-jnp.arange(0, ROPE_DIM, 2, dtype=F32) / ROPE_DIM)
    ang = pos[:, None] * inv_freq[None, :]
    cos, sin = jnp.cos(ang), jnp.sin(ang)
    rest = HEAD - ROPE_DIM
    c = jnp.concatenate([cos, cos, jnp.ones((T, rest), F32)], axis=1)
    s_next = jnp.concatenate([-sin, jnp.zeros((T, half + rest), F32)], axis=1)
    s_prev = jnp.concatenate([jnp.zeros((T, half), F32), sin, jnp.zeros((T, rest), F32)], axis=1)
    return tuple(jnp.tile(t, (1, PAIR)) for t in (c, s_next, s_prev))


def _qkv_kernel(x_ref, g_ref, w_ref, b_ref, c_ref, sn_ref, sp_ref, o_ref):
    half = ROPE_DIM // 2
    qk_dim = D_MODEL + KV_DIM
    reps = qk_dim // LANES
    h = _rms(x_ref[...], g_ref[...])
    qkv = _dot(h, w_ref[...]) + b_ref[...]
    qk = qkv[:, :qk_dim]
    c = jnp.concatenate([c_ref[...]] * reps, axis=1)
    s_next = jnp.concatenate([sn_ref[...]] * reps, axis=1)
    s_prev = jnp.concatenate([sp_ref[...]] * reps, axis=1)
    rot = (qk * c + pltpu.roll(qk, qk_dim - half, 1) * s_next + pltpu.roll(qk, half, 1) * s_prev)
    o_ref[:, :D_MODEL] = (rot[:, :D_MODEL] * ATTN_Q_SCALE).astype(o_ref.dtype)
    o_ref[:, D_MODEL:qk_dim] = rot[:, D_MODEL:].astype(o_ref.dtype)
    o_ref[:, qk_dim:] = qkv[:, qk_dim:].astype(o_ref.dtype)


def _qkv_proj(x, g, w, b, tables, T):
    M, D = x.shape
    tm = QKV_TILE
    tiles_per_seq = T // tm
    tile = pl.BlockSpec((tm, D), lambda i: (i, 0))
    tab = pl.BlockSpec((tm, LANES), lambda i: (i % tiles_per_seq, 0))
    consts = [g, w, b]
    return pl.pallas_call(
        _qkv_kernel,
        grid=(M // tm,),
        in_specs=[tile] + [_layer_spec(p) for p in consts] + [tab] * 3,
        out_specs=pl.BlockSpec((tm, QKV_DIM), lambda i: (i, 0)),
        out_shape=jax.ShapeDtypeStruct((M, QKV_DIM), BF16),
        compiler_params=_params("parallel"),
        name="attn_qkv_rope",
    )(x, *[p.stack for p in consts], *tables)


def _attn_kernel(layer, sinks_ref, q_ref, kp_ref, kc_ref, vp_ref, vc_ref, x_ref, w_ref, b_ref,
                 g_ref, o_ref):
    first = pl.program_id(1) * ATTN_BLOCKS
    rows = PAIR * BLOCK
    qi = lax.broadcasted_iota(jnp.int32, (rows, 2 * BLOCK), 0) % BLOCK
    kj = lax.broadcasted_iota(jnp.int32, (rows, 2 * BLOCK), 1)
    rel = qi + BLOCK - kj
    band = (rel >= 0) & (rel < WINDOW)
    top = lax.broadcasted_iota(jnp.int32, (rows, 1), 0) < BLOCK
    lane_lo = lax.broadcasted_iota(jnp.int32, (1, LANES), 1) < HEAD

    def both_halves(tile):
        swapped = pltpu.roll(tile, HEAD, 1)
        return jnp.where(lane_lo, tile, swapped), jnp.where(lane_lo, swapped, tile)

    keys = jnp.concatenate([kp_ref[0], kc_ref[0]], axis=0).astype(F32)
    vals = jnp.concatenate([vp_ref[0], vc_ref[0]], axis=0).astype(F32)
    ones = jnp.ones((keys.shape[0], LANES), BF16)
    k_heads, v_heads = [], []
    for t in range(KV_DIM // LANES):
        k_heads += [x.astype(BF16) for x in both_halves(keys[:, t * LANES:(t + 1) * LANES])]
        v_heads += [jnp.concatenate([x.astype(BF16), ones], axis=1)
                    for x in both_halves(vals[:, t * LANES:(t + 1) * LANES])]

    sink_slot =lax.broadcasted_iota(jnp.int32, (rows, LANES), 1) == 0
    slot_row = (lax.broadcasted_iota(jnp.int32, (2 * BLOCK, 2 * LANES), 0) == 0) & (
        lax.broadcasted_iota(jnp.int32, (2 * BLOCK, 2 * LANES), 1) < LANES)
    no_key = jnp.full((rows, LANES), -jnp.inf, F32)
    v_win = [[jnp.where(slot_row, jnp.zeros((), BF16), vh[blk * BLOCK:(blk + 2) * BLOCK])
              for vh in v_heads] for blk in range(ATTN_BLOCKS)]

    def head_pairs(work):
        masked = []
        for blk, t in work:
            qt = q_ref[0, blk * BLOCK:(blk + 1) * BLOCK, t * LANES:(t + 1) * LANES].astype(F32)
            q2 = jnp.concatenate([jnp.where(lane_lo, qt, 0.0), jnp.where(lane_lo, 0.0, qt)], axis=0)
            hk = (t * PAIR) // GROUP
            s = _dot_nt(q2, k_heads[hk][blk * BLOCK:(blk + 2) * BLOCK])
            sink = jnp.where(top, sinks_ref[layer, t * PAIR],
                             sinks_ref[layer, t * PAIR + 1]) * LOG2E
            fill = jnp.concatenate([jnp.where(sink_slot, sink, -jnp.inf), no_key], axis=1)
            valid = band & ((first + blk > 0) | (kj >= BLOCK))
            masked.append(jnp.where(valid, s, fill))
        mx = [jnp.max(x, axis=-1, keepdims=True) for x in masked]
        p = [jnp.exp2(masked[i] - mx[i]) for i in range(len(work))]
        pv = [_dot(p[i], v_win[blk][(t * PAIR) // GROUP])
              for i, (blk, t) in enumerate(work)]
        out = []
        for i in range(len(work)):
            o2 = pv[i][:, :LANES] / pv[i][:, LANES:]
            out.append(jnp.where(lane_lo, o2[:BLOCK], o2[BLOCK:]))
        return out

    work = [(blk, t) for blk in range(ATTN_BLOCKS) for t in range(D_MODEL // LANES)]
    tiles = []
    for i in range(0, len(work), ATTN_OVERLAP):
        tiles += head_pairs(work[i:i + ATTN_OVERLAP])
    n_t = D_MODEL // LANES
    o = jnp.concatenate([jnp.concatenate(tiles[blk * n_t:(blk + 1) * n_t], axis=1)
                         for blk in range(ATTN_BLOCKS)], axis=0)
    out = _dot(o, w_ref[...]) + b_ref[...]
    o_ref[0] = x_ref[0] + _rms(out, g_ref[...])


def _attention(qkv, x, sinks, w_o, b_o, g):
    B, T, D = x.shape
    rows = ATTN_BLOCKS * BLOCK
    k_blk = D_MODEL // KV_DIM
    v_blk = k_blk + 1
    cur = lambda b, n: (b, n, 0)
    prev = lambda lane_blk: (lambda b, n: (b, jnp.maximum(n * ATTN_BLOCKS - 1, 0), lane_blk))
    consts = [w_o, b_o, g]
    return pl.pallas_call(
        functools.partial(_attn_kernel, sinks.index),
        grid=(B, T // rows),
        in_specs=[
            pl.BlockSpec(memory_space=pltpu.SMEM),
            pl.BlockSpec((1, rows, D), cur),
            pl.BlockSpec((1, BLOCK, KV_DIM), prev(k_blk)),
            pl.BlockSpec((1, rows, KV_DIM), lambda b, n: (b, n, k_blk)),
            pl.BlockSpec((1, BLOCK, KV_DIM), prev(v_blk)),
            pl.BlockSpec((1, rows, KV_DIM), lambda b, n: (b, n, v_blk)),
            pl.BlockSpec((1, rows, D), cur),
        ] + [_layer_spec(p) for p in consts],
        out_specs=pl.BlockSpec((1, rows, D), cur),
        out_shape=jax.ShapeDtypeStruct((B, T, D), F32),
        compiler_params=_params("parallel", "arbitrary"),
        name="swa_sink_attention",
    )(sinks.stack, qkv, qkv, qkv, qkv, qkv, x, *[p.stack for p in consts])


def kernel(x, norm_g, rwkv_x_mix, rwkv_w_rkv, rwkv_w0, rwkv_w1, rwkv_w2, rwkv_a0, rwkv_a1,
           rwkv_a2, rwkv_v0, rwkv_v1, rwkv_v2, rwkv_g1, rwkv_g2, rwkv_k_k, rwkv_k_a, rwkv_r_k,
           rwkv_lnx_w, rwkv_lnx_b, rwkv_w_o, attn_w_qkv, attn_b_qkv, attn_sinks, attn_w_o,
           attn_b_o, ffn_w_gate_up, ffn_w_down):
    B, T, D = x.shape
    M = B * T
    depth = norm_g.shape[0]
    tables = _rope_tables(T)
    v_first = None
    x = x.reshape(M, D)
    gains = norm_g.reshape(depth * 4, D)
    r_k = rwkv_r_k.reshape(rwkv_r_k.shape[0], D)
    for i in range(depth):
        j = i // 2
        gain = lambda which: _layer(gains, 4 * i + which)
        mixer_out = None
        if i % 2 == 0:
            vres = None if j == 0 else (v_first, _layer(rwkv_v0, j - 1), _layer(rwkv_v1, j - 1),
                                        _layer(rwkv_v2, j - 1))
            r, lw, k, v, asig, gate = _rwkv_proj(
                x.reshape(B, T, D), gain(0), *[_layer(p, j) for p in (
                    rwkv_x_mix, rwkv_w_rkv, rwkv_w0, rwkv_w1, rwkv_w2, rwkv_a0, rwkv_a1, rwkv_a2,
                    rwkv_g1, rwkv_g2)], vres)
            if j == 0:
                v_first = v
            y = _wkv(r, lw, k, v, asig, *[_layer(p, j) for p in (
                rwkv_k_k, rwkv_k_a, r_k, rwkv_lnx_w, rwkv_lnx_b)])
            mixer_out = (y.reshape(M, D), gate.reshape(M, D), _layer(rwkv_w_o, j), gain(1))
        else:
            qkv = _qkv_proj(x, gain(0), _layer(attn_w_qkv, j), _layer(attn_b_qkv, j), tables, T)
            x = _attention(qkv.reshape(B, T, QKV_DIM), x.reshape(B, T, D), _Layer(attn_sinks, j),
                           _layer(attn_w_o, j), _layer(attn_b_o, j), gain(1)).reshape(M, D)
        x = _ffn(x, mixer_out, gain(2), _layer(ffn_w_gate_up, i), _layer(ffn_w_down, i), gain(3))
    return x.reshape(B, T, D)
```

```python
import functools
import math
from typing import NamedTuple

import jax
import jax.numpy as jnp
from jax import lax
from jax.experimental import pallas as pl
from jax.experimental.pallas import tpu as pltpu

D_MODEL = 1024
HEAD = 64
N_HEADS = D_MODEL // HEAD
N_KV_HEADS = 4
GROUP = N_HEADS // N_KV_HEADS
KV_DIM = N_KV_HEADS * HEAD
QKV_DIM = D_MODEL + 2 * KV_DIM
WINDOW = 128
BLOCK = 128
ROPE_THETA = 500000.0
ROPE_DIM = HEAD // 4
D_FF = 2816
RMS_EPS = 1e-6
GN_EPS = 64e-5
LOG2E = math.log2(math.e)
ATTN_Q_SCALE = HEAD ** -0.5 * LOG2E

LANES = 128
SUBLANES = 8
PAIR = LANES // HEAD
WKV_CHUNK = 64
WKV_GROUP_W = 128
WKV_BATCH = 4
ATTN_BLOCKS = 4
ATTN_OVERLAP = 8
TOKEN_TILE = 512
QKV_TILE = 1024
FFN_TILE = 512
FFN_CHUNK = 512
VMEM_LIMIT = 56 * 1024 * 1024

BF16 = jnp.bfloat16
F32 = jnp.float32

assert WKV_CHUNK == HEAD
assert WINDOW <= BLOCK


def _dot(a, b):
    return jnp.dot(a.astype(BF16), b.astype(BF16), preferred_element_type=F32)


def _dot_nt(a, b):
    return lax.dot_general(a.astype(BF16), b.astype(BF16), (((1,), (1,)), ((), ())),
                           preferred_element_type=F32)


def _dot_tn(a, b):
    return lax.dot_general(a.astype(BF16), b.astype(BF16), (((0,), (0,)), ((), ())),
                           preferred_element_type=F32)


def _cumsum_rows(x, tri01):
    hi = x.astype(BF16)
    lo = (x - hi.astype(F32)).astype(BF16)
    return jnp.dot(jnp.concatenate([tri01, tri01], axis=1), jnp.concatenate([hi, lo], axis=0),
                   preferred_element_type=F32)


def _rms(x, g):
    return x * lax.rsqrt(jnp.mean(x * x, axis=-1, keepdims=True) + RMS_EPS) * g


def _sigmoid(x):
    return 1.0 / (1.0 + jnp.exp(-x))


class _Layer(NamedTuple):
    stack: jax.Array
    index: int


def _layer(stack, index):
    if stack.ndim == 2:
        stack = stack.reshape(stack.shape[0], 1, stack.shape[1])
    return _Layer(stack, index)


def _layer_spec(p):
    shape = p.stack.shape[1:]
    where = (p.index,) + (0,) * len(shape)
    return pl.BlockSpec((None,) + shape, lambda *_: where, pipeline_mode=pl.Buffered(1))


def _params(*sem):
    return pltpu.CompilerParams(dimension_semantics=sem, vmem_limit_bytes=VMEM_LIMIT)


def _rwkv_proj_kernel(has_vres, *refs):
    if has_vres:
        (x_ref, halo_ref, g_ref, mix_ref, wrkv_ref, w0_ref, w1_ref, w2_ref, a0_ref, a1_ref,
         a2_ref, g1_ref, g2_ref, vf_ref, v0_ref, v1_ref, v2_ref,
         r_ref, lw_ref, k_ref, v_ref, as_ref, gate_ref) = refs
    else:
        (x_ref, halo_ref, g_ref, mix_ref, wrkv_ref, w0_ref, w1_ref, w2_ref, a0_ref, a1_ref,
         a2_ref, g1_ref, g2_ref,
         r_ref, lw_ref, k_ref, v_ref, as_ref, gate_ref) = refs
    i = pl.program_id(1)
    g = g_ref[...]
    h = _rms(x_ref[0], g)
    prev_last = _rms(halo_ref[0][SUBLANES - 1:, :], g)
    prev_last = jnp.where(i == 0, 0.0, prev_last)
    row = lax.broadcasted_iota(jnp.int32, h.shape, 0)
    h_prev = jnp.where(row == 0, prev_last, pltpu.roll(h, 1, 0))
    xx = h_prev - h
    mix = mix_ref[...]
    xr, xk, xv, xw, xa, xg = (h + xx * mix[j:j + 1, :] for j in range(6))

    r_ref[0] = _dot(xr, wrkv_ref[0]).astype(r_ref.dtype)
    k_ref[0] = _dot(xk, wrkv_ref[1]).astype(k_ref.dtype)
    v = _dot(xv, wrkv_ref[2])
    if has_vres:
        mixv = _sigmoid(v0_ref[...] + _dot(_dot(xv, v1_ref[...]), v2_ref[...]))
        v = v + (vf_ref[0].astype(F32) - v) * mixv
    v_ref[0] = v.astype(v_ref.dtype)
    z = w0_ref[...] + _dot(jnp.tanh(_dot(xw, w1_ref[...])), w2_ref[...])
    lw_ref[0] = (-math.exp(-0.5)) * _sigmoid(z)
    a_sig = _sigmoid(a0_ref[...] + _dot(_dot(xa, a1_ref[...]), a2_ref[...]))
    as_ref[0] = a_sig.astype(as_ref.dtype)
    gate_ref[0] = _dot(_sigmoid(_dot(xg, g1_ref[...])), g2_ref[...]).astype(gate_ref.dtype)


def _rwkv_proj(x, g, mix, wrkv, w0, w1, w2, a0, a1, a2, g1, g2, vres):
    B, T, D = x.shape
    tm = TOKEN_TILE
    tile = pl.BlockSpec((1, tm, D), lambda b, i: (b, i, 0))
    halo = pl.BlockSpec((1, SUBLANES, D),
                        lambda b, i: (b, jnp.maximum(i * (tm // SUBLANES) - 1, 0), 0))
    layer = [g, mix, wrkv, w0, w1, w2, a0, a1, a2, g1, g2]
    args = [x, x] + [p.stack for p in layer]
    specs = [tile, halo] + [_layer_spec(p) for p in layer]
    if vres is not None:
        v_first, v0, v1, v2 = vres
        args += [v_first] + [p.stack for p in (v0, v1, v2)]
        specs += [tile] + [_layer_spec(p) for p in (v0, v1, v2)]
    out = [jax.ShapeDtypeStruct((B, T, D), dt) for dt in (BF16, F32, BF16, BF16, BF16, BF16)]
    return pl.pallas_call(
        functools.partial(_rwkv_proj_kernel, vres is not None),
        grid=(B, T // tm),
        in_specs=specs,
        out_specs=[tile] * 6,
        out_shape=out,
        compiler_params=_params("parallel", "arbitrary"),
        name="rwkv_proj",
    )(*args)


def _wkv_kernel(r_ref, lw_ref, k_ref, v_ref, as_ref, kk_ref, ka_ref, rk_ref, lnw_ref, lnb_ref,
                y_ref, state_ref):
    L = WKV_CHUNK
    W = WKV_GROUP_W
    groups = D_MODEL // W
    chains = [(bi, q) for bi in range(WKV_BATCH) for q in range(groups)]
    n = len(chains)
    n_double = int(math.log2(L))

    @pl.when(pl.program_id(1) == 0)
    def _():
        state_ref[...] = jnp.zeros_like(state_ref)

    ti = lax.broadcasted_iota(jnp.int32, (L, L), 0)
    tj = lax.broadcasted_iota(jnp.int32, (L, L), 1)
    tri_incl = (ti >= tj).astype(BF16)
    step_row = lax.broadcasted_iota(jnp.int32, (L, W), 0)
    step_col = lax.broadcasted_iota(jnp.int32, (L, W), 1) % L
    strict_lower = step_row > step_col
    lower = step_row >= step_col
    diagonal = step_row == step_col
    head_i = lax.broadcasted_iota(jnp.int32, (W, W), 0) // HEAD
    head_j = lax.broadcasted_iota(jnp.int32, (W, W), 1) // HEAD
    same_head = head_i == head_j
    head_ones = same_head.astype(BF16)

    def blockdiag(x):
        xb = x.astype(BF16)
        return jnp.where(same_head, jnp.concatenate([xb] * (W // L), axis=0), jnp.zeros((), BF16))

    def headsum(xs):
        out = jnp.dot(jnp.concatenate(xs, axis=0).astype(BF16), head_ones,
                      preferred_element_type=F32)
        return [out[i * L:(i + 1) * L] for i in range(len(xs))]

    def load(ref, c):
        return ref[c[0], :, c[1] * W:(c[1] + 1) * W].astype(F32)

    def vec(ref, c):
        return ref[:, c[1] * W:(c[1] + 1) * W]

    r = [load(r_ref, c) for c in chains]
    lw = [load(lw_ref, c) for c in chains]
    k_in = [load(k_ref, c) for c in chains]
    v = [load(v_ref, c) for c in chains]
    asig = [load(as_ref, c) for c in chains]

    kk = [k_in[i] * vec(kk_ref, c) for i, c in enumerate(chains)]
    norm2 = headsum([x * x for x in kk])
    kk = [x / jnp.maximum(jnp.sqrt(s), 1e-12) for x, s in zip(kk, norm2)]
    k = [k_in[i] * (1.0 + (asig[i] - 1.0) * vec(ka_ref, c)) for i, c in enumerate(chains)]

    cum = [_cumsum_rows(x, tri_incl) for x in lw]
    decay_to = [jnp.exp(x) for x in cum]
    decay_end = [x[L - 1:L, :] for x in decay_to]
    inv = [jnp.exp(-x) for x in cum]
    ar = [jnp.concatenate([-kk[i] * jnp.exp(cum[i] - lw[i]), r[i] * decay_to[i]],
                          axis=0).astype(BF16) for i in range(n)]
    b_s = [kk[i] * asig[i] * inv[i] for i in range(n)]
    k_s = [k[i] * inv[i] for i in range(n)]
    v_bd = [blockdiag(x) for x in v]

    s_b = [_dot_nt(ar[i], blockdiag(b_s[i])) for i in range(n)]
    s_k = [_dot_nt(ar[i], blockdiag(k_s[i])) for i in range(n)]
    s0 = [state_ref[i] for i in range(n)]
    from_state = [_dot_nt(ar[i], s0[i]) for i in range(n)]

    apow = [jnp.where(strict_lower, x[:L], 0.0) for x in s_b]
    tinv = [jnp.where(diagonal, 1.0, a) for a in apow]
    apow = [_dot(a, blockdiag(a)) for a in apow]
    for step in range(1, n_double):
        power_bd = [blockdiag(a) for a in apow]
        if step + 1 < n_double:
            both = [_dot(jnp.concatenate([apow[i], tinv[i]], axis=0), power_bd[i])
                    for i in range(n)]
            apow = [x[:L] for x in both]
            tinv = [tinv[i] + both[i][L:] for i in range(n)]
        else:
            tinv = [tinv[i] + _dot(tinv[i], power_bd[i]) for i in range(n)]

    with_v = [_dot(jnp.concatenate([jnp.where(strict_lower, s_k[i][:L], 0.0),
                                    jnp.where(lower, s_k[i][L:], 0.0)], axis=0), v_bd[i])
              for i in range(n)]
    u = [_dot(tinv[i], blockdiag(from_state[i][:L] + with_v[i][:L])) for i in range(n)]
    y = [from_state[i][L:] + with_v[i][L:]
         + _dot(jnp.where(lower, s_b[i][L:], 0.0), blockdiag(u[i])) for i in range(n)]
    for i in range(n):
        uv = jnp.concatenate([u[i], v[i]], axis=0)
        bk_end = jnp.concatenate([b_s[i], k_s[i]], axis=0) * decay_end[i]
        state_ref[i] = s0[i] * decay_end[i] + jnp.where(same_head, _dot_tn(uv, bk_end), 0.0)

    mu = headsum(y)
    yc = [y[i] - mu[i] * (1.0 / HEAD) for i in range(n)]
    var = headsum([x * x for x in yc])
    rk = headsum([r[i] * k[i] * vec(rk_ref, c) for i, c in enumerate(chains)])
    for i, c in enumerate(chains):
        yn = yc[i] * lax.rsqrt(var[i] * (1.0 / HEAD) + GN_EPS) * vec(lnw_ref, c) + vec(lnb_ref, c)
        y_ref[c[0], :, c[1] * W:(c[1] + 1) * W] = (yn + rk[i] * v[i]).astype(y_ref.dtype)


def _wkv(r, lw, k, v, asig, k_k, k_a, r_k, lnx_w, lnx_b):
    B, T, D = r.shape
    L = WKV_CHUNK
    seq = pl.BlockSpec((WKV_BATCH, L, D), lambda b, c: (b, c, 0))
    vecs = [k_k, k_a, r_k, lnx_w, lnx_b]
    n_chains = WKV_BATCH * (D // WKV_GROUP_W)
    return pl.pallas_call(
        _wkv_kernel,
        grid=(B // WKV_BATCH, T // L),
        in_specs=[seq] * 5 + [_layer_spec(p) for p in vecs],
        out_specs=seq,
        out_shape=jax.ShapeDtypeStruct((B, T, D), BF16),
        scratch_shapes=[pltpu.VMEM((n_chains, WKV_GROUP_W, WKV_GROUP_W), F32)],
        compiler_params=_params("parallel", "arbitrary"),
        name="wkv7_chunked",
    )(r, lw, k, v, asig, *[p.stack for p in vecs])


def _ffn_kernel(has_mixer_out, *refs):
    if has_mixer_out:
        (x_ref, y_ref, gate_ref, wo_ref, gmix_ref,
         gin_ref, wgu_ref, wd_ref, gout_ref, o_ref) = refs
        mixed = y_ref[...].astype(F32) * gate_ref[...].astype(F32)
        x = x_ref[...] + _rms(_dot(mixed, wo_ref[...]), gmix_ref[...])
    else:
        x_ref, gin_ref, wgu_ref, wd_ref, gout_ref, o_ref = refs
        x = x_ref[...]
    h = _rms(x, gin_ref[...]).astype(BF16)
    acc = jnp.zeros(x.shape, F32)
    for lo in range(0, D_FF, FFN_CHUNK):
        hi = min(lo + FFN_CHUNK, D_FF)
        gate = _dot(h, wgu_ref[:, lo:hi])
        up = _dot(h, wgu_ref[:, D_FF + lo:D_FF + hi])
        act = gate * _sigmoid(gate) * up
        acc = acc + _dot(act, wd_ref[lo:hi, :])
    o_ref[...] = x + _rms(acc, gout_ref[...])


def _ffn(x, mixer_out, g_in, w_gate_up, w_down, g_out):
    M, D = x.shape
    tm = FFN_TILE
    tile = pl.BlockSpec((tm, D), lambda i: (i, 0))
    acts, consts = [x], []
    if mixer_out is not None:
        y, gate, w_o, g_mix = mixer_out
        acts += [y, gate]
        consts += [w_o, g_mix]
    consts += [g_in, w_gate_up, w_down, g_out]
    return pl.pallas_call(
        functools.partial(_ffn_kernel, mixer_out is not None),
        grid=(M // tm,),
        in_specs=[tile] * len(acts) + [_layer_spec(p) for p in consts],
        out_specs=tile,
        out_shape=jax.ShapeDtypeStruct((M, D), F32),
        compiler_params=_params("parallel"),
        name="swiglu_ffn",
    )(*acts, *[p.stack for p in consts])


def _rope_tables(T):
    half = ROPE_DIM // 2
    pos = jnp.arange(T, dtype=F32)
    inv_freq = ROPE_THETA ** (-jnp.arange(0, ROPE_DIM, 2, dtype=F32) / ROPE_DIM)
    ang = pos[:, None] * inv_freq[None, :]
    cos, sin = jnp.cos(ang), jnp.sin(ang)
    rest = HEAD - ROPE_DIM
    c = jnp.concatenate([cos, cos, jnp.ones((T, rest), F32)], axis=1)
    s_next = jnp.concatenate([-sin, jnp.zeros((T, half + rest), F32)], axis=1)
    s_prev = jnp.concatenate([jnp.zeros((T, half), F32), sin, jnp.zeros((T, rest), F32)], axis=1)
    return tuple(jnp.tile(t, (1, PAIR)) for t in (c, s_next, s_prev))


def _qkv_kernel(x_ref, g_ref, w_ref, b_ref, c_ref, sn_ref, sp_ref, o_ref):
    half = ROPE_DIM // 2
    qk_dim = D_MODEL + KV_DIM
    reps = qk_dim // LANES
    h = _rms(x_ref[...], g_ref[...])
    qkv = _dot(h, w_ref[...]) + b_ref[...]
    qk = qkv[:, :qk_dim]
    c = jnp.concatenate([c_ref[...]] * reps, axis=1)
    s_next = jnp.concatenate([sn_ref[...]] * reps, axis=1)
    s_prev = jnp.concatenate([sp_ref[...]] * reps, axis=1)
    rot = (qk * c + pltpu.roll(qk, qk_dim - half, 1) * s_next + pltpu.roll(qk, half, 1) * s_prev)
    o_ref[:, :D_MODEL] = (rot[:, :D_MODEL] * ATTN_Q_SCALE).astype(o_ref.dtype)
    o_ref[:, D_MODEL:qk_dim] = rot[:, D_MODEL:].astype(o_ref.dtype)
    o_ref[:, qk_dim:] = qkv[:, qk_dim:].astype(o_ref.dtype)


def _qkv_proj(x, g, w, b, tables, T):
    M, D = x.shape
    tm = QKV_TILE
    assert T % tm == 0 and M % tm == 0
    tiles_per_seq = T // tm
    tile = pl.BlockSpec((tm, D), lambda i: (i, 0))
    tab = pl.BlockSpec((tm, LANES), lambda i: (i % tiles_per_seq, 0))
    consts = [g, w, b]
    return pl.pallas_call(
        _qkv_kernel,
        grid=(M // tm,),
        in_specs=[tile] + [_layer_spec(p) for p in consts] + [tab] * 3,
        out_specs=pl.BlockSpec((tm, QKV_DIM), lambda i: (i, 0)),
        out_shape=jax.ShapeDtypeStruct((M, QKV_DIM), BF16),
        compiler_params=_params("parallel"),
        name="attn_qkv_rope",
    )(x, *[p.stack for p in consts], *tables)


def _attn_kernel(layer, sinks_ref, q_ref, kp_ref, kc_ref, vp_ref, vc_ref, x_ref, w_ref, b_ref,
                 g_ref, o_ref):
    first = pl.program_id(1) * ATTN_BLOCKS
    rows = PAIR * BLOCK
    qi = lax.broadcasted_iota(jnp.int32, (rows, 2 * BLOCK), 0) % BLOCK
    kj = lax.broadcasted_iota(jnp.int32, (rows, 2 * BLOCK), 1)
    rel = qi + BLOCK - kj
    band = (rel >= 0) & (rel < WINDOW)
    top = lax.broadcasted_iota(jnp.int32, (rows, 1), 0) < BLOCK
    lane_lo = lax.broadcasted_iota(jnp.int32, (1, LANES), 1) < HEAD

    def both_halves(tile):
        swapped = pltpu.roll(tile, HEAD, 1)
        return jnp.where(lane_lo, tile, swapped), jnp.where(lane_lo, swapped, tile)

    keys = jnp.concatenate([kp_ref[0], kc_ref[0]], axis=0).astype(F32)
    vals = jnp.concatenate([vp_ref[0], vc_ref[0]], axis=0).astype(F32)
    ones = jnp.ones((keys.shape[0], LANES), BF16)
    k_heads, v_heads = [], []
    for t in range(KV_DIM // LANES):
        k_heads += [x.astype(BF16) for x in both_halves(keys[:, t * LANES:(t + 1) * LANES])]
        v_heads += [jnp.concatenate([x.astype(BF16), ones], axis=1)
                    for x in both_halves(vals[:, t * LANES:(t + 1) * LANES])]

    sink_slot =lax.broadcasted_iota(jnp.int32, (rows, LANES), 1) == 0
    slot_row = (lax.broadcasted_iota(jnp.int32, (2 * BLOCK, 2 * LANES), 0) == 0) & (
        lax.broadcasted_iota(jnp.int32, (2 * BLOCK, 2 * LANES), 1) < LANES)
    no_key = jnp.full((rows, LANES), -jnp.inf, F32)
    v_win = [[jnp.where(slot_row, jnp.zeros((), BF16), vh[blk * BLOCK:(blk + 2) * BLOCK])
              for vh in v_heads] for blk in range(ATTN_BLOCKS)]

    def head_pairs(work):
        masked = []
        for blk, t in work:
            qt = q_ref[0, blk * BLOCK:(blk + 1) * BLOCK, t * LANES:(t + 1) * LANES].astype(F32)
            q2 = jnp.concatenate([jnp.where(lane_lo, qt, 0.0), jnp.where(lane_lo, 0.0, qt)], axis=0)
            hk = (t * PAIR) // GROUP
            s = _dot_nt(q2, k_heads[hk][blk * BLOCK:(blk + 2) * BLOCK])
            sink = jnp.where(top, sinks_ref[layer, t * PAIR],
                             sinks_ref[layer, t * PAIR + 1]) * LOG2E
            fill = jnp.concatenate([jnp.where(sink_slot, sink, -jnp.inf), no_key], axis=1)
            valid = band & ((first + blk > 0) | (kj >= BLOCK))
            masked.append(jnp.where(valid, s, fill))
        mx = [jnp.max(x, axis=-1, keepdims=True) for x in masked]
        p = [jnp.exp2(masked[i] - mx[i]) for i in range(len(work))]
        pv = [_dot(p[i], v_win[blk][(t * PAIR) // GROUP])
              for i, (blk, t) in enumerate(work)]
        out = []
        for i in range(len(work)):
            o2 = pv[i][:, :LANES] / pv[i][:, LANES:]
            out.append(jnp.where(lane_lo, o2[:BLOCK], o2[BLOCK:]))
        return out

    work = [(blk, t) for blk in range(ATTN_BLOCKS) for t in range(D_MODEL // LANES)]
    tiles = []
    for i in range(0, len(work), ATTN_OVERLAP):
        tiles += head_pairs(work[i:i + ATTN_OVERLAP])
    n_t = D_MODEL // LANES
    o = jnp.concatenate([jnp.concatenate(tiles[blk * n_t:(blk + 1) * n_t], axis=1)
                         for blk in range(ATTN_BLOCKS)], axis=0)
    out = _dot(o, w_ref[...]) + b_ref[...]
    o_ref[0] = x_ref[0] + _rms(out, g_ref[...])


def _attention(qkv, x, sinks, w_o, b_o, g):
    B, T, D = x.shape
    rows = ATTN_BLOCKS * BLOCK
    k_blk = D_MODEL // KV_DIM
    v_blk = k_blk + 1
    cur = lambda b, n: (b, n, 0)
    prev = lambda lane_blk: (lambda b, n: (b, jnp.maximum(n * ATTN_BLOCKS - 1, 0), lane_blk))
    consts = [w_o, b_o, g]
    return pl.pallas_call(
        functools.partial(_attn_kernel, sinks.index),
        grid=(B, T // rows),
        in_specs=[
            pl.BlockSpec(memory_space=pltpu.SMEM),
            pl.BlockSpec((1, rows, D), cur),
            pl.BlockSpec((1, BLOCK, KV_DIM), prev(k_blk)),
            pl.BlockSpec((1, rows, KV_DIM), lambda b, n: (b, n, k_blk)),
            pl.BlockSpec((1, BLOCK, KV_DIM), prev(v_blk)),
            pl.BlockSpec((1, rows, KV_DIM), lambda b, n: (b, n, v_blk)),
            pl.BlockSpec((1, rows, D), cur),
        ] + [_layer_spec(p) for p in consts],
        out_specs=pl.BlockSpec((1, rows, D), cur),
        out_shape=jax.ShapeDtypeStruct((B, T, D), F32),
        compiler_params=_params("parallel", "arbitrary"),
        name="swa_sink_attention",
    )(sinks.stack, qkv, qkv, qkv, qkv, qkv, x, *[p.stack for p in consts])


def kernel(x, norm_g, rwkv_x_mix, rwkv_w_rkv, rwkv_w0, rwkv_w1, rwkv_w2, rwkv_a0, rwkv_a1,
           rwkv_a2, rwkv_v0, rwkv_v1, rwkv_v2, rwkv_g1, rwkv_g2, rwkv_k_k, rwkv_k_a, rwkv_r_k,
           rwkv_lnx_w, rwkv_lnx_b, rwkv_w_o, attn_w_qkv, attn_b_qkv, attn_sinks, attn_w_o,
           attn_b_o, ffn_w_gate_up, ffn_w_down):
    B, T, D = x.shape
    M = B * T
    depth = norm_g.shape[0]
    tables = _rope_tables(T)
    v_first = None
    x = x.reshape(M, D)
    gains = norm_g.reshape(depth * 4, D)
    r_k = rwkv_r_k.reshape(rwkv_r_k.shape[0], D)
    for i in range(depth):
        j = i // 2
        gain = lambda which: _layer(gains, 4 * i + which)
        mixer_out = None
        if i % 2 == 0:
            vres = None if j == 0 else (v_first, _layer(rwkv_v0, j - 1), _layer(rwkv_v1, j - 1),
                                        _layer(rwkv_v2, j - 1))
            r, lw, k, v, asig, gate = _rwkv_proj(
                x.reshape(B, T, D), gain(0), *[_layer(p, j) for p in (
                    rwkv_x_mix, rwkv_w_rkv, rwkv_w0, rwkv_w1, rwkv_w2, rwkv_a0, rwkv_a1, rwkv_a2,
                    rwkv_g1, rwkv_g2)], vres)
            if j == 0:
                v_first = v
            y = _wkv(r, lw, k, v, asig, *[_layer(p, j) for p in (
                rwkv_k_k, rwkv_k_a, r_k, rwkv_lnx_w, rwkv_lnx_b)])
            mixer_out = (y.reshape(M, D), gate.reshape(M, D), _layer(rwkv_w_o, j), gain(1))
        else:
            qkv = _qkv_proj(x, gain(0), _layer(attn_w_qkv, j), _layer(attn_b_qkv, j), tables, T)
            x = _attention(qkv.reshape(B, T, QKV_DIM), x.reshape(B, T, D), _Layer(attn_sinks, j),
                           _layer(attn_w_o, j), _layer(attn_b_o, j), gain(1)).reshape(M, D)
        x = _ffn(x, mixer_out, gain(2), _layer(ffn_w_gate_up, i), _layer(ffn_w_down, i), gain(3))
    return x.reshape(B, T, D)
```

```python
import functools
import math
from typing import NamedTuple

import jax
import jax.numpy as jnp
from jax import lax
from jax.experimental import pallas as pl
from jax.experimental.pallas import tpu as pltpu

D_MODEL = 1024
HEAD = 64
N_HEADS = D_MODEL // HEAD
N_KV_HEADS = 4
GROUP = N_HEADS // N_KV_HEADS
KV_DIM = N_KV_HEADS * HEAD
QKV_DIM = D_MODEL + 2 * KV_DIM
WINDOW = 128
BLOCK = 128
ROPE_THETA = 500000.0
ROPE_DIM = HEAD // 4
D_FF = 2816
RMS_EPS = 1e-6
GN_EPS = 64e-5
LOG2E = math.log2(math.e)
ATTN_Q_SCALE = HEAD ** -0.5 * LOG2E

LANES = 128
SUBLANES = 8
PAIR = LANES // HEAD
WKV_CHUNK = 64
WKV_GROUP_W = 128
WKV_BATCH = 4
ATTN_BLOCKS = 4
ATTN_OVERLAP = 16
TOKEN_TILE = 512
FFN_TILE = 512
FFN_CHUNK = 512
VMEM_LIMIT = 56 * 1024 * 1024

BF16 = jnp.bfloat16
F32 = jnp.float32

assert WKV_CHUNK == HEAD
assert WINDOW <= BLOCK


def _dot(a, b):
    return jnp.dot(a.astype(BF16), b.astype(BF16), preferred_element_type=F32)


def _dot_nt(a, b):
    return lax.dot_general(a.astype(BF16), b.astype(BF16), (((1,), (1,)), ((), ())),
                           preferred_element_type=F32)


def _dot_tn(a, b):
    return lax.dot_general(a.astype(BF16), b.astype(BF16), (((0,), (0,)), ((), ())),
                           preferred_element_type=F32)


def _cumsum_rows(x, tri01):
    hi = x.astype(BF16)
    lo = (x - hi.astype(F32)).astype(BF16)
    return jnp.dot(jnp.concatenate([tri01, tri01], axis=1), jnp.concatenate([hi, lo], axis=0),
                   preferred_element_type=F32)


def _rms(x, g):
    return x * lax.rsqrt(jnp.mean(x * x, axis=-1, keepdims=True) + RMS_EPS) * g


def _sigmoid(x):
    return 1.0 / (1.0 + jnp.exp(-x))


class _Layer(NamedTuple):
    stack: jax.Array
    index: int


def _layer(stack, index):
    if stack.ndim == 2:
        stack = stack.reshape(stack.shape[0], 1, stack.shape[1])
    return _Layer(stack, index)


def _layer_spec(p):
    shape = p.stack.shape[1:]
    where = (p.index,) + (0,) * len(shape)
    return pl.BlockSpec((None,) + shape, lambda *_: where, pipeline_mode=pl.Buffered(1))


def _params(*sem):
    return pltpu.CompilerParams(dimension_semantics=sem, vmem_limit_bytes=VMEM_LIMIT)


def _rwkv_proj_kernel(has_vres, *refs):
    if has_vres:
        (x_ref, halo_ref, g_ref, mix_ref, wrkv_ref, w0_ref, w1_ref, w2_ref, a0_ref, a1_ref,
         a2_ref, g1_ref, g2_ref, vf_ref, v0_ref, v1_ref, v2_ref,
         r_ref, lw_ref, k_ref, v_ref, as_ref, gate_ref) = refs
    else:
        (x_ref, halo_ref, g_ref, mix_ref, wrkv_ref, w0_ref, w1_ref, w2_ref, a0_ref, a1_ref,
         a2_ref, g1_ref, g2_ref,
         r_ref, lw_ref, k_ref, v_ref, as_ref, gate_ref) = refs
    i = pl.program_id(1)
    g = g_ref[...]
    h = _rms(x_ref[0], g)
    prev_last = _rms(halo_ref[0][SUBLANES - 1:, :], g)
    prev_last = jnp.where(i == 0, 0.0, prev_last)
    row = lax.broadcasted_iota(jnp.int32, h.shape, 0)
    h_prev = jnp.where(row == 0, prev_last, pltpu.roll(h, 1, 0))
    xx = h_prev - h
    mix = mix_ref[...]
    xr, xk, xv, xw, xa, xg = (h + xx * mix[j:j + 1, :] for j in range(6))

    r_ref[0] = _dot(xr, wrkv_ref[0]).astype(r_ref.dtype)
    k_ref[0] = _dot(xk, wrkv_ref[1]).astype(k_ref.dtype)
    v = _dot(xv, wrkv_ref[2])
    if has_vres:
        mixv = _sigmoid(v0_ref[...] + _dot(_dot(xv, v1_ref[...]), v2_ref[...]))
        v = v + (vf_ref[0].astype(F32) - v) * mixv
    v_ref[0] = v.astype(v_ref.dtype)
    z = w0_ref[...] + _dot(jnp.tanh(_dot(xw, w1_ref[...])), w2_ref[...])
    lw_ref[0] = (-math.exp(-0.5)) * _sigmoid(z)
    a_sig = _sigmoid(a0_ref[...] + _dot(_dot(xa, a1_ref[...]), a2_ref[...]))
    as_ref[0] = a_sig.astype(as_ref.dtype)
    gate_ref[0] = _dot(_sigmoid(_dot(xg, g1_ref[...])), g2_ref[...]).astype(gate_ref.dtype)


def _rwkv_proj(x, g, mix, wrkv, w0, w1, w2, a0, a1, a2, g1, g2, vres):
    B, T, D = x.shape
    tm = TOKEN_TILE
    tile = pl.BlockSpec((1, tm, D), lambda b, i: (b, i, 0))
    halo = pl.BlockSpec((1, SUBLANES, D),
                        lambda b, i: (b, jnp.maximum(i * (tm // SUBLANES) - 1, 0), 0))
    layer = [g, mix, wrkv, w0, w1, w2, a0, a1, a2, g1, g2]
    args = [x, x] + [p.stack for p in layer]
    specs = [tile, halo] + [_layer_spec(p) for p in layer]
    if vres is not None:
        v_first, v0, v1, v2 = vres
        args += [v_first] + [p.stack for p in (v0, v1, v2)]
        specs += [tile] + [_layer_spec(p) for p in (v0, v1, v2)]
    out = [jax.ShapeDtypeStruct((B, T, D), dt) for dt in (BF16, F32, BF16, BF16, BF16, BF16)]
    return pl.pallas_call(
        functools.partial(_rwkv_proj_kernel, vres is not None),
        grid=(B, T // tm),
        in_specs=specs,
        out_specs=[tile] * 6,
        out_shape=out,
        compiler_params=_params("parallel", "arbitrary"),
        name="rwkv_proj",
    )(*args)


def _wkv_kernel(r_ref, lw_ref, k_ref, v_ref, as_ref, kk_ref, ka_ref, rk_ref, lnw_ref, lnb_ref,
                y_ref, state_ref):
    L = WKV_CHUNK
    W = WKV_GROUP_W
    groups = D_MODEL // W
    chains = [(bi, q) for bi in range(WKV_BATCH) for q in range(groups)]
    n = len(chains)
    n_double = int(math.log2(L))

    @pl.when(pl.program_id(1) == 0)
    def _():
        state_ref[...] = jnp.zeros_like(state_ref)

    ti = lax.broadcasted_iota(jnp.int32, (L, L), 0)
    tj = lax.broadcasted_iota(jnp.int32, (L, L), 1)
    tri_incl = (ti >= tj).astype(BF16)
    step_row = lax.broadcasted_iota(jnp.int32, (L, W), 0)
    step_col = lax.broadcasted_iota(jnp.int32, (L, W), 1) % L
    strict_lower = step_row > step_col
    lower = step_row >= step_col
    diagonal = step_row == step_col
    head_i = lax.broadcasted_iota(jnp.int32, (W, W), 0) // HEAD
    head_j = lax.broadcasted_iota(jnp.int32, (W, W), 1) // HEAD
    same_head = head_i == head_j
    head_ones = same_head.astype(BF16)

    def blockdiag(x):
        xb = x.astype(BF16)
        return jnp.where(same_head, jnp.concatenate([xb] * (W // L), axis=0), jnp.zeros((), BF16))

    def headsum(xs):
        out = jnp.dot(jnp.concatenate(xs, axis=0).astype(BF16), head_ones,
                      preferred_element_type=F32)
        return [out[i * L:(i + 1) * L] for i in range(len(xs))]

    def load(ref, c):
        return ref[c[0], :, c[1] * W:(c[1] + 1) * W].astype(F32)

    def vec(ref, c):
        return ref[:, c[1] * W:(c[1] + 1) * W]

    r = [load(r_ref, c) for c in chains]
    lw = [load(lw_ref, c) for c in chains]
    k_in = [load(k_ref, c) for c in chains]
    v = [load(v_ref, c) for c in chains]
    asig = [load(as_ref, c) for c in chains]

    kk = [k_in[i] * vec(kk_ref, c) for i, c in enumerate(chains)]
    norm2 = headsum([x * x for x in kk])
    kk = [x / jnp.maximum(jnp.sqrt(s), 1e-12) for x, s in zip(kk, norm2)]
    k = [k_in[i] * (1.0 + (asig[i] - 1.0) * vec(ka_ref, c)) for i, c in enumerate(chains)]

    cum = [_cumsum_rows(x, tri_incl) for x in lw]
    decay_to = [jnp.exp(x) for x in cum]
    decay_end = [x[L - 1:L, :] for x in decay_to]
    inv = [jnp.exp(-x) for x in cum]
    ar = [jnp.concatenate([-kk[i] * jnp.exp(cum[i] - lw[i]), r[i] * decay_to[i]],
                          axis=0).astype(BF16) for i in range(n)]
    b_s = [kk[i] * asig[i] * inv[i] for i in range(n)]
    k_s = [k[i] * inv[i] for i in range(n)]
    v_bd = [blockdiag(x) for x in v]

    s_b = [_dot_nt(ar[i], blockdiag(b_s[i])) for i in range(n)]
    s_k = [_dot_nt(ar[i], blockdiag(k_s[i])) for i in range(n)]
    s0 = [state_ref[i] for i in range(n)]
    from_state = [_dot_nt(ar[i], s0[i]) for i in range(n)]

    apow = [jnp.where(strict_lower, x[:L], 0.0) for x in s_b]
    tinv = [jnp.where(diagonal, 1.0, a) for a in apow]
    apow = [_dot(a, blockdiag(a)) for a in apow]
    for step in range(1, n_double):
        power_bd = [blockdiag(a) for a in apow]
        if step + 1 < n_double:
            both = [_dot(jnp.concatenate([apow[i], tinv[i]], axis=0), power_bd[i])
                    for i in range(n)]
            apow = [x[:L] for x in both]
            tinv = [tinv[i] + both[i][L:] for i in range(n)]
        else:
            tinv = [tinv[i] + _dot(tinv[i], power_bd[i]) for i in range(n)]

    with_v = [_dot(jnp.concatenate([jnp.where(strict_lower, s_k[i][:L], 0.0),
                                    jnp.where(lower, s_k[i][L:], 0.0)], axis=0), v_bd[i])
              for i in range(n)]
    u = [_dot(tinv[i], blockdiag(from_state[i][:L] + with_v[i][:L])) for i in range(n)]
    y = [from_state[i][L:] + with_v[i][L:]
         + _dot(jnp.where(lower, s_b[i][L:], 0.0), blockdiag(u[i])) for i in range(n)]
    for i in range(n):
        uv = jnp.concatenate([u[i], v[i]], axis=0)
        bk_end = jnp.concatenate([b_s[i], k_s[i]], axis=0) * decay_end[i]
        state_ref[i] = s0[i] * decay_end[i] + jnp.where(same_head, _dot_tn(uv, bk_end), 0.0)

    mu = headsum(y)
    yc = [y[i] - mu[i] * (1.0 / HEAD) for i in range(n)]
    var = headsum([x * x for x in yc])
    rk = headsum([r[i] * k[i] * vec(rk_ref, c) for i, c in enumerate(chains)])
    for i, c in enumerate(chains):
        yn = yc[i] * lax.rsqrt(var[i] * (1.0 / HEAD) + GN_EPS) * vec(lnw_ref, c) + vec(lnb_ref, c)
        y_ref[c[0], :, c[1] * W:(c[1] + 1) * W] = (yn + rk[i] * v[i]).astype(y_ref.dtype)


def _wkv(r, lw, k, v, asig, k_k, k_a, r_k, lnx_w, lnx_b):
    B, T, D = r.shape
    L = WKV_CHUNK
    seq = pl.BlockSpec((WKV_BATCH, L, D), lambda b, c: (b, c, 0))
    vecs = [k_k, k_a, r_k, lnx_w, lnx_b]
    n_chains = WKV_BATCH * (D // WKV_GROUP_W)
    return pl.pallas_call(
        _wkv_kernel,
        grid=(B // WKV_BATCH, T // L),
        in_specs=[seq] * 5 + [_layer_spec(p) for p in vecs],
        out_specs=seq,
        out_shape=jax.ShapeDtypeStruct((B, T, D), BF16),
        scratch_shapes=[pltpu.VMEM((n_chains, WKV_GROUP_W, WKV_GROUP_W), F32)],
        compiler_params=_params("parallel", "arbitrary"),
        name="wkv7_chunked",
    )(r, lw, k, v, asig, *[p.stack for p in vecs])


def _ffn_kernel(has_mixer_out, *refs):
    if has_mixer_out:
        (x_ref, y_ref, gate_ref, wo_ref, gmix_ref,
         gin_ref, wgu_ref, wd_ref, gout_ref, o_ref) = refs
        mixed = y_ref[...].astype(F32) * gate_ref[...].astype(F32)
        x = x_ref[...] + _rms(_dot(mixed, wo_ref[...]), gmix_ref[...])
    else:
        x_ref, gin_ref, wgu_ref, wd_ref, gout_ref, o_ref = refs
        x = x_ref[...]
    h = _rms(x, gin_ref[...]).astype(BF16)
    acc = jnp.zeros(x.shape, F32)
    for lo in range(0, D_FF, FFN_CHUNK):
        hi = min(lo + FFN_CHUNK, D_FF)
        gate = _dot(h, wgu_ref[:, lo:hi])
        up = _dot(h, wgu_ref[:, D_FF + lo:D_FF + hi])
        act = gate * _sigmoid(gate) * up
        acc = acc + _dot(act, wd_ref[lo:hi, :])
    o_ref[...] = x + _rms(acc, gout_ref[...])


def _ffn(x, mixer_out, g_in, w_gate_up, w_down, g_out):
    M, D = x.shape
    tm = FFN_TILE
    tile = pl.BlockSpec((tm, D), lambda i: (i, 0))
    acts, consts = [x], []
    if mixer_out is not None:
        y, gate, w_o, g_mix = mixer_out
        acts += [y, gate]
        consts += [w_o, g_mix]
    consts += [g_in, w_gate_up, w_down, g_out]
    return pl.pallas_call(
        functools.partial(_ffn_kernel, mixer_out is not None),
        grid=(M // tm,),
        in_specs=[tile] * len(acts) + [_layer_spec(p) for p in consts],
        out_specs=tile,
        out_shape=jax.ShapeDtypeStruct((M, D), F32),
        compiler_params=_params("parallel"),
        name="swiglu_ffn",
    )(*acts, *[p.stack for p in consts])


def _rope_tables(T):
    half = ROPE_DIM // 2
    pos = jnp.arange(T, dtype=F32)
    inv_freq = ROPE_THETA ** (-jnp.arange(0, ROPE_DIM, 2, dtype=F32) / ROPE_DIM)
    ang = pos[:, None] * inv_freq[None, :]
    cos, sin = jnp.cos(ang), jnp.sin(ang)
    rest = HEAD - ROPE_DIM
    c = jnp.concatenate([cos, cos, jnp.ones((T, rest), F32)], axis=1)
    s_next = jnp.concatenate([-sin, jnp.zeros((T, half + rest), F32)], axis=1)
    s_prev = jnp.concatenate([jnp.zeros((T, half), F32), sin, jnp.zeros((T, rest), F32)], axis=1)
    return tuple(jnp.tile(t, (1, PAIR)) for t in (c, s_next, s_prev))


def _attn_kernel(layer, sinks_ref, x_ref, gpre_ref, wqkv_ref, bqkv_ref, c_ref, sn_ref, sp_ref,
                 w_ref, b_ref, g_ref, o_ref, kv_prev_ref):
    first = pl.program_id(1) * ATTN_BLOCKS

    @pl.when(first == 0)
    def _():
        kv_prev_ref[...] = jnp.zeros_like(kv_prev_ref)

    half = ROPE_DIM // 2

    def rope(t, row0):
        n, width = t.shape
        coef = [jnp.concatenate([r[row0:row0 + n, :]] * (width // LANES), axis=1)
                for r in (c_ref, sn_ref, sp_ref)]
        return (t * coef[0] + pltpu.roll(t, width - half, 1) * coef[1]
                + pltpu.roll(t, half, 1) * coef[2])

    h = _rms(x_ref[0], gpre_ref[...]).astype(BF16)
    kv_cur = _dot(h, wqkv_ref[:, D_MODEL:]) + bqkv_ref[:, D_MODEL:]
    kv_cur = jnp.concatenate([rope(kv_cur[:, :KV_DIM], 0), kv_cur[:, KV_DIM:]], axis=1)
    kv_all = jnp.concatenate([kv_prev_ref[...], kv_cur], axis=0)
    kv_prev_ref[...] = kv_cur[kv_cur.shape[0] - BLOCK:]

    def q_block(blk):
        q = (_dot(h[blk * BLOCK:(blk + 1) * BLOCK], wqkv_ref[:, :D_MODEL])
             + bqkv_ref[:, :D_MODEL])
        return (rope(q, blk * BLOCK) * ATTN_Q_SCALE).astype(BF16)
    rows = PAIR * BLOCK
    qi = lax.broadcasted_iota(jnp.int32, (rows, 2 * BLOCK), 0) % BLOCK
    kj = lax.broadcasted_iota(jnp.int32, (rows, 2 * BLOCK), 1)
    rel = qi + BLOCK - kj
    band = (rel >= 0) & (rel < WINDOW)
    top = lax.broadcasted_iota(jnp.int32, (rows, 1), 0) < BLOCK
    lane_lo = lax.broadcasted_iota(jnp.int32, (1, LANES), 1) < HEAD

    def both_halves(tile):
        swapped = pltpu.roll(tile, HEAD, 1)
        return jnp.where(lane_lo, tile, swapped), jnp.where(lane_lo, swapped, tile)

    keys = kv_all[:, :KV_DIM]
    vals = kv_all[:, KV_DIM:]
    ones = jnp.ones((keys.shape[0], LANES), BF16)
    k_heads, v_heads = [], []
    for t in range(KV_DIM // LANES):
        k_heads += [x.astype(BF16) for x in both_halves(keys[:, t * LANES:(t + 1) * LANES])]
        v_heads += [jnp.concatenate([x.astype(BF16), ones], axis=1)
                    for x in both_halves(vals[:, t * LANES:(t + 1) * LANES])]

    sink_slot =lax.broadcasted_iota(jnp.int32, (rows, LANES), 1) == 0
    slot_row = (lax.broadcasted_iota(jnp.int32, (2 * BLOCK, 2 * LANES), 0) == 0) & (
        lax.broadcasted_iota(jnp.int32, (2 * BLOCK, 2 * LANES), 1) < LANES)
    no_key = jnp.full((rows, LANES), -jnp.inf, F32)
    v_win = [[jnp.where(slot_row, jnp.zeros((), BF16), vh[blk * BLOCK:(blk + 2) * BLOCK])
              for vh in v_heads] for blk in range(ATTN_BLOCKS)]

    def head_pairs(work):
        masked = []
        q_of = {blk: q_block(blk) for blk in sorted({blk for blk, _ in work})}
        for blk, t in work:
            qt = q_of[blk][:, t * LANES:(t + 1) * LANES].astype(F32)
            q2 = jnp.concatenate([jnp.where(lane_lo, qt, 0.0), jnp.where(lane_lo, 0.0, qt)], axis=0)
            hk = (t * PAIR) // GROUP
            s = _dot_nt(q2, k_heads[hk][blk * BLOCK:(blk + 2) * BLOCK])
            sink = jnp.where(top, sinks_ref[layer, t * PAIR],
                             sinks_ref[layer, t * PAIR + 1]) * LOG2E
            fill = jnp.concatenate([jnp.where(sink_slot, sink, -jnp.inf), no_key], axis=1)
            valid = band & ((first + blk > 0) | (kj >= BLOCK))
            masked.append(jnp.where(valid, s, fill))
        mx = [jnp.max(x, axis=-1, keepdims=True) for x in masked]
        p = [jnp.exp2(masked[i] - mx[i]) for i in range(len(work))]
        pv = [_dot(p[i], v_win[blk][(t * PAIR) // GROUP])
              for i, (blk, t) in enumerate(work)]
        out = []
        for i in range(len(work)):
            o2 = pv[i][:, :LANES] / pv[i][:, LANES:]
            out.append(jnp.where(lane_lo, o2[:BLOCK], o2[BLOCK:]))
        return out

    work = [(blk, t) for blk in range(ATTN_BLOCKS) for t in range(D_MODEL // LANES)]
    tiles = []
    for i in range(0, len(work), ATTN_OVERLAP):
        tiles += head_pairs(work[i:i + ATTN_OVERLAP])
    n_t = D_MODEL // LANES
    o = jnp.concatenate([jnp.concatenate(tiles[blk * n_t:(blk + 1) * n_t], axis=1)
                         for blk in range(ATTN_BLOCKS)], axis=0)
    out = _dot(o, w_ref[...]) + b_ref[...]
    o_ref[0] = x_ref[0] + _rms(out, g_ref[...])


def _attention(x, sinks, g_pre, w_qkv, b_qkv, tables, w_o, b_o, g):
    B, T, D = x.shape
    rows = ATTN_BLOCKS * BLOCK
    assert T % rows == 0
    cur = lambda b, n: (b, n, 0)
    tab = pl.BlockSpec((rows, LANES), lambda b, n: (n, 0))
    pre, post = [g_pre, w_qkv, b_qkv], [w_o, b_o, g]
    return pl.pallas_call(
        functools.partial(_attn_kernel, sinks.index),
        grid=(B, T // rows),
        in_specs=[pl.BlockSpec(memory_space=pltpu.SMEM), pl.BlockSpec((1, rows, D), cur)]
        + [_layer_spec(p) for p in pre] + [tab] * 3 + [_layer_spec(p) for p in post],
        out_specs=pl.BlockSpec((1, rows, D), cur),
        out_shape=jax.ShapeDtypeStruct((B, T, D), F32),
        scratch_shapes=[pltpu.VMEM((BLOCK, 2 * KV_DIM), F32)],
        compiler_params=_params("parallel", "arbitrary"),
        name="swa_sink_attention",
    )(sinks.stack, x, *[p.stack for p in pre], *tables, *[p.stack for p in post])


def kernel(x, norm_g, rwkv_x_mix, rwkv_w_rkv, rwkv_w0, rwkv_w1, rwkv_w2, rwkv_a0, rwkv_a1,
           rwkv_a2, rwkv_v0, rwkv_v1, rwkv_v2, rwkv_g1, rwkv_g2, rwkv_k_k, rwkv_k_a, rwkv_r_k,
           rwkv_lnx_w, rwkv_lnx_b, rwkv_w_o, attn_w_qkv, attn_b_qkv, attn_sinks, attn_w_o,
           attn_b_o, ffn_w_gate_up, ffn_w_down):
    B, T, D = x.shape
    M = B * T
    depth = norm_g.shape[0]
    tables = _rope_tables(T)
    v_first = None
    x = x.reshape(M, D)
    gains = norm_g.reshape(depth * 4, D)
    r_k = rwkv_r_k.reshape(rwkv_r_k.shape[0], D)
    for i in range(depth):
        j = i // 2
        gain = lambda which: _layer(gains, 4 * i + which)
        mixer_out = None
        if i % 2 == 0:
            vres = None if j == 0 else (v_first, _layer(rwkv_v0, j - 1), _layer(rwkv_v1, j - 1),
                                        _layer(rwkv_v2, j - 1))
            r, lw, k, v, asig, gate = _rwkv_proj(
                x.reshape(B, T, D), gain(0), *[_layer(p, j) for p in (
                    rwkv_x_mix, rwkv_w_rkv, rwkv_w0, rwkv_w1, rwkv_w2, rwkv_a0, rwkv_a1, rwkv_a2,
                    rwkv_g1, rwkv_g2)], vres)
            if j == 0:
                v_first = v
            y = _wkv(r, lw, k, v, asig, *[_layer(p, j) for p in (
                rwkv_k_k, rwkv_k_a, r_k, rwkv_lnx_w, rwkv_lnx_b)])
            mixer_out = (y.reshape(M, D), gate.reshape(M, D), _layer(rwkv_w_o, j), gain(1))
        else:
            x = _attention(x.reshape(B, T, D), _Layer(attn_sinks, j), gain(0),
                           _layer(attn_w_qkv, j), _layer(attn_b_qkv, j), tables,
                           _layer(attn_w_o, j), _layer(attn_b_o, j), gain(1)).reshape(M, D)
        x = _ffn(x, mixer_out, gain(2), _layer(ffn_w_gate_up, i), _layer(ffn_w_down, i), gain(3))
    return x.reshape(B, T, D)
```
